```python
import math
import jax, jax.numpy as jnp
from jax import lax
import numpy as np

D_MODEL = 2048
BATCH = 16
SEQ = 2048
DEPTH = 4

CHUNK = 64
MIX_WIDTH = D_MODEL
RET_WIDTH = MIX_WIDTH // 2
RET_HEADS = 4
RET_HEAD_DIM = RET_WIDTH // RET_HEADS
ATT_WIDTH = MIX_WIDTH - RET_WIDTH
ATT_HEAD_DIM = 64
ATT_HEADS = ATT_WIDTH // ATT_HEAD_DIM
N_PREV_CHUNKS = 8
N_BAND_CHUNKS = N_PREV_CHUNKS + 1
BAND = N_BAND_CHUNKS * CHUNK
REL_CLIP = 256
REL_TABLE = REL_CLIP + CHUNK
IN_WIDTH = 4 * RET_WIDTH + 3 * ATT_WIDTH
SPLITS = [RET_WIDTH, 2 * RET_WIDTH, 3 * RET_WIDTH, 4 * RET_WIDTH,
          4 * RET_WIDTH + ATT_WIDTH, 4 * RET_WIDTH + 2 * ATT_WIDTH]
N_EXPERTS = 16
N_GROUPS = 4
EXPERTS_PER_GROUP = N_EXPERTS // N_GROUPS
TOP_K = 2
D_FF = D_MODEL // 2
ROPE_BASE = 10000.0
LN_EPS = 1e-5
DEEPNORM_ALPHA = (2 * DEPTH) ** 0.25
DEEPNORM_BETA = (8 * DEPTH) ** -0.25

kernel_name = "hybrid_retention_chunkattn_grouped_moe_deepnorm"

F32 = jnp.float32


def layer_norm(x, g, b):
    xf = x.astype(F32)
    mu = xf.mean(-1, keepdims=True)
    var = jnp.square(xf - mu).mean(-1, keepdims=True)
    return ((xf - mu) * lax.rsqrt(var + LN_EPS) * g.astype(F32) + b.astype(F32)).astype(x.dtype)


def rotary(t):
    s, d = t.shape[1], t.shape[-1]
    half = d // 2
    inv = ROPE_BASE ** (-jnp.arange(half, dtype=F32) / half)
    ang = jnp.arange(s, dtype=F32)[:, None] * inv[None, :]
    cos = jnp.cos(ang)[None, :, None, :].astype(t.dtype)
    sin = jnp.sin(ang)[None, :, None, :].astype(t.dtype)
    t1, t2 = t[..., :half], t[..., half:]
    return jnp.concatenate([t1 * cos - t2 * sin, t1 * sin + t2 * cos], axis=-1)


def retention(q, k, v, g, norm_gain):
    b, s, h, dk = q.shape
    dv = v.shape[-1]
    nc = s // CHUNK
    dt = q.dtype
    log_gamma = jnp.log1p(-jnp.exp2(-5.0 - jnp.arange(h, dtype=F32)))
    q = rotary(q)
    k = rotary(k) * (dk ** -0.5)
    idx = jnp.arange(CHUNK, dtype=F32)
    intra_decay = jnp.exp(log_gamma[:, None, None] * jnp.abs(idx[:, None] - idx[None, :])).astype(dt)
    key_decay = jnp.exp((CHUNK - 1 - idx)[:, None] * log_gamma[None, :]).astype(dt)
    qry_decay = jnp.exp((idx + 1)[:, None] * log_gamma[None, :]).astype(dt)
    chunk_decay = jnp.exp(log_gamma * CHUNK)
    qc = q.reshape(b, nc, CHUNK, h, dk)
    kc = k.reshape(b, nc, CHUNK, h, dk)
    vc = v.reshape(b, nc, CHUNK, h, dv)
    scores = jnp.einsum('bnqhd,bnkhd->bnhqk', qc, kc) * intra_decay
    intra = jnp.einsum('bnhqk,bnkhe->bnqhe', scores, vc)

    def step(state, xs):
        q_n, k_n, v_n = xs
        cross = jnp.einsum('bqhd,bhde->bqhe', q_n * qry_decay[None, :, :, None], state.astype(dt))
        kv = jnp.einsum('bkhd,bkhe->bhde', k_n * key_decay[None, :, :, None], v_n)
        state = state * chunk_decay[None, :, None, None] + kv.astype(F32)
        return state, cross

    init = jnp.zeros((b, h, dk, dv), F32)
    to_chunk_major = lambda t: jnp.swapaxes(t, 0, 1)
    _, cross = lax.scan(step, init, (to_chunk_major(qc), to_chunk_major(kc), to_chunk_major(vc)))
    o = (intra + to_chunk_major(cross)).reshape(b, s, h, dv)
    of = o.astype(F32)
    mu = of.mean(-1, keepdims=True)
    var = jnp.square(of - mu).mean(-1, keepdims=True)
    on = ((of - mu) * lax.rsqrt(var + LN_EPS)).reshape(b, s, h * dv) * norm_gain.astype(F32)
    return jax.nn.silu(g.reshape(b, s, h * dv)) * on.astype(dt)


def chunk_band_attention(q, k, v, rel_table):
    b, s, h, dh = q.shape
    nc = s // CHUNK
    dt = q.dtype
    qc = q.reshape(b, nc, CHUNK, h, dh)
    pad = ((0, 0), (N_PREV_CHUNKS, 0), (0, 0), (0, 0), (0, 0))
    kp = jnp.pad(k.reshape(b, nc, CHUNK, h, dh), pad)
    vp = jnp.pad(v.reshape(b, nc, CHUNK, h, dh), pad)
    band_idx = np.arange(nc)[:, None] + np.arange(N_BAND_CHUNKS)[None, :]
    kb = kp[:, band_idx].reshape(b, nc, BAND, h, dh)
    vb = vp[:, band_idx].reshape(b, nc, BAND, h, dh)
    scores = jnp.einsum('bnqhd,bnkhd->bnhqk', qc, kb).astype(F32) * (dh ** -0.5)
    dist = N_PREV_CHUNKS * CHUNK + np.arange(CHUNK)[:, None] - np.arange(BAND)[None, :]
    rel_idx = np.clip(dist, -(CHUNK - 1), REL_CLIP) + (CHUNK - 1)
    bias = rel_table[:, rel_idx].astype(F32)
    valid = (np.arange(nc)[:, None] - N_PREV_CHUNKS + np.arange(N_BAND_CHUNKS)[None, :]) >= 0
    valid = np.repeat(valid, CHUNK, axis=1)
    scores = scores + bias[None, None]
    scores = jnp.where(valid[None, :, None, None, :], scores, jnp.finfo(F32).min)
    p = jax.nn.softmax(scores, axis=-1).astype(dt)
    o = jnp.einsum('bnhqk,bnkhd->bnqhd', p, vb)
    return o.reshape(b, s, h * dh)


def grouped_top2_moe(x, router_w, router_b, w_gate, w_up, w_down):
    b, s, d = x.shape
    dt = x.dtype
    t = x.reshape(b * s, d)
    affinity = jax.nn.sigmoid(jnp.dot(t, router_w).astype(F32))
    sel = affinity + router_b.astype(F32)[None, :]
    grp_score = lax.top_k(sel.reshape(-1, N_GROUPS, EXPERTS_PER_GROUP), TOP_K)[0].sum(-1)
    grp_choice = jnp.argmax(grp_score, axis=-1)
    expert_group = jnp.arange(N_EXPERTS) // EXPERTS_PER_GROUP
    in_group = expert_group[None, :] == grp_choice[:, None]
    _, top_idx = lax.top_k(jnp.where(in_group, sel, -jnp.inf), TOP_K)
    top_aff = jnp.take_along_axis(affinity, top_idx, axis=-1)
    top_w = top_aff / top_aff.sum(-1, keepdims=True)
    combine = jnp.sum(jax.nn.one_hot(top_idx, N_EXPERTS, dtype=F32) * top_w[..., None], axis=1).astype(dt)
    y = jnp.zeros_like(t)
    for e in range(N_EXPERTS):
        hdn = jax.nn.silu(jnp.dot(t, w_gate[e])) * jnp.dot(t, w_up[e])
        y = y + combine[:, e:e + 1] * jnp.dot(hdn, w_down[e])
    return y.reshape(b, s, d)


def setup_inputs(seed: int = 0) -> dict:
    key = jax.random.key(seed)
    ks = jax.random.split(key, 16)
    nrm = lambda k, shape: jax.random.normal(k, shape, F32)
    x = nrm(ks[0], (BATCH, SEQ, D_MODEL))
    col_scale = jnp.concatenate([
        jnp.ones((2 * RET_WIDTH,), F32), jnp.full((RET_WIDTH,), DEEPNORM_BETA, F32),
        jnp.ones((RET_WIDTH + 2 * ATT_WIDTH,), F32), jnp.full((ATT_WIDTH,), DEEPNORM_BETA, F32)])
    w_in = nrm(ks[1], (DEPTH, D_MODEL, IN_WIDTH)) * (D_MODEL ** -0.5) * col_scale
    ret_norm_gain = 1.0 + 0.02 * nrm(ks[2], (DEPTH, RET_WIDTH))
    rel_bias = 0.1 * nrm(ks[3], (DEPTH, ATT_HEADS, REL_TABLE))
    w_out = nrm(ks[4], (DEPTH, MIX_WIDTH, D_MODEL)) * (MIX_WIDTH ** -0.5) * DEEPNORM_BETA
    ln1_g = 1.0 + 0.02 * nrm(ks[5], (DEPTH, D_MODEL))
    ln1_b = 0.02 * nrm(ks[6], (DEPTH, D_MODEL))
    router_w = nrm(ks[7], (D_MODEL, N_EXPERTS)) * (D_MODEL ** -0.5)
    router_b = 0.01 * nrm(ks[8], (N_EXPERTS,))
    w_gate = nrm(ks[9], (DEPTH, N_EXPERTS, D_MODEL, D_FF)) * (D_MODEL ** -0.5)
    w_up = nrm(ks[10], (DEPTH, N_EXPERTS, D_MODEL, D_FF)) * (D_MODEL ** -0.5) * DEEPNORM_BETA
    w_down = nrm(ks[11], (DEPTH, N_EXPERTS, D_FF, D_MODEL)) * (D_FF ** -0.5) * DEEPNORM_BETA
    ln2_g = 1.0 + 0.02 * nrm(ks[12], (DEPTH, D_MODEL))
    ln2_b = 0.02 * nrm(ks[13], (DEPTH, D_MODEL))
    return {"x": x, "w_in": w_in, "ret_norm_gain": ret_norm_gain, "rel_bias": rel_bias,
            "w_out": w_out, "ln1_g": ln1_g, "ln1_b": ln1_b, "router_w": router_w,
            "router_b": router_b, "w_gate": w_gate, "w_up": w_up, "w_down": w_down,
            "ln2_g": ln2_g, "ln2_b": ln2_b}


def reference(x, w_in, ret_norm_gain, rel_bias, w_out, ln1_g, ln1_b, router_w, router_b,
              w_gate, w_up, w_down, ln2_g, ln2_b):
    b, s, _ = x.shape
    for l in range(DEPTH):
        proj = jnp.dot(x, w_in[l])
        rq, rk, rv, rg, aq, ak, av = jnp.split(proj, SPLITS, axis=-1)
        rh = lambda t: t.reshape(b, s, RET_HEADS, RET_HEAD_DIM)
        ah = lambda t: t.reshape(b, s, ATT_HEADS, ATT_HEAD_DIM)
        ret_out = retention(rh(rq), rh(rk), rh(rv), rh(rg), ret_norm_gain[l])
        att_out = chunk_band_attention(ah(aq), ah(ak), ah(av), rel_bias[l])
        mixed = jnp.concatenate([ret_out, att_out], axis=-1)
        x = layer_norm(DEEPNORM_ALPHA * x + jnp.dot(mixed, w_out[l]), ln1_g[l], ln1_b[l])
        moe = grouped_top2_moe(x, router_w, router_b, w_gate[l], w_up[l], w_down[l])
        x = layer_norm(DEEPNORM_ALPHA * x + moe, ln2_g[l], ln2_b[l])
    return x
```

```python
import functools

import jax
import jax.numpy as jnp
import numpy as np
from jax import lax
from jax.experimental import pallas as pl
from jax.experimental.pallas import tpu as pltpu

F32 = jnp.float32
BF16 = jnp.bfloat16
I32 = jnp.int32
U32 = jnp.uint32

D_MODEL = 2048
DEPTH = 4
CHUNK = 64
RET_WIDTH = 1024
RET_HEADS = 4
RET_HEAD_DIM = 256
ATT_WIDTH = 1024
ATT_HEAD_DIM = 64
ATT_HEADS = 16
N_PREV_CHUNKS = 8
REL_CLIP = 256
IN_WIDTH = 4 * RET_WIDTH + 3 * ATT_WIDTH
N_EXPERTS = 16
N_GROUPS = 4
EXPERTS_PER_GROUP = 4
D_FF = 1024
ROPE_BASE = 10000.0
LN_EPS = 1e-5
DEEPNORM_ALPHA = (2 * DEPTH) ** 0.25

LANES = 128
MXU_DIM = 256
VMEM_LIMIT_BYTES = 56 * 1024 * 1024

SUPER = 4 * CHUNK
ATT_GROUP = MXU_DIM // ATT_HEAD_DIM
KEY_SPAN = 3 * SUPER
HALF_ROW = D_MODEL // 2
MASK_VALUE = -1e30

INPROJ_TM = 1024
INPROJ_TN = 1024
OUTPROJ_TM = 256
RANK_TT = 512
DISPATCH_TT = 256
FFN_TM = 512
COMBINE_TT = 256


def _params(semantics):
    return pltpu.CompilerParams(dimension_semantics=semantics, vmem_limit_bytes=VMEM_LIMIT_BYTES)


def _layer_norm(y, g, b):
    mu = jnp.mean(y, axis=-1, keepdims=True)
    yc = y - mu
    var = jnp.mean(yc * yc, axis=-1, keepdims=True)
    return yc * lax.rsqrt(var + LN_EPS) * g + b


def _pack_rows(y):
    yb = y.astype(BF16).astype(F32)
    lo = lax.bitcast_convert_type(yb[:, :HALF_ROW], U32)
    hi = lax.bitcast_convert_type(yb[:, HALF_ROW:], U32)
    return (lo >> 16) | (hi & jnp.uint32(0xFFFF0000))


def _unpack_rows(w):
    lo = lax.bitcast_convert_type(w << 16, F32)
    hi = lax.bitcast_convert_type(w & jnp.uint32(0xFFFF0000), F32)
    return lo, hi


def _inproj_kernel(x_ref, w_ref, o_ref, xb_ref):
    @pl.when(pl.program_id(1) == 0)
    def _():
        xb_ref[...] = x_ref[...].astype(BF16)

    o_ref[...] = jnp.dot(xb_ref[...], w_ref[...], preferred_element_type=F32).astype(BF16)


def _inproj(x2d, w_bf):
    t = x2d.shape[0]
    tm = min(INPROJ_TM, t)
    return pl.pallas_call(
        _inproj_kernel,
        out_shape=jax.ShapeDtypeStruct((t, IN_WIDTH), BF16),
        grid=(t // tm, IN_WIDTH // INPROJ_TN),
        in_specs=[pl.BlockSpec((tm, D_MODEL), lambda i, j: (i, 0)),
                  pl.BlockSpec((D_MODEL, INPROJ_TN), lambda i, j: (0, j))],
        out_specs=pl.BlockSpec((tm, INPROJ_TN), lambda i, j: (i, j)),
        scratch_shapes=[pltpu.VMEM((tm, D_MODEL), BF16)],
        compiler_params=_params(("parallel", "arbitrary")),
        name="inproj",
    )(x2d, w_bf)


def _rotate(t, cos, sin):
    t1, t2 = t[:, :LANES], t[:, LANES:]
    return t1 * cos - t2 * sin, t1 * sin + t2 * cos


def _retention_kernel(q_ref, k_ref, v_ref, g_ref, cos_ref, sin_ref, dmat_ref, qd_ref, kd_ref, cd_ref,
                      gain_ref, o_ref, state_ref):
    @pl.when(pl.program_id(2) == 0)
    def _():
        state_ref[...] = jnp.zeros_like(state_ref)

    cos, sin = cos_ref[...], sin_ref[...]
    q1, q2 = _rotate(q_ref[...].astype(F32), cos, sin)
    k1, k2 = _rotate(k_ref[...].astype(F32), cos, sin)
    scale = RET_HEAD_DIM ** -0.5
    k1, k2 = k1 * scale, k2 * scale
    qd, kd = qd_ref[...], kd_ref[...]
    v = v_ref[...]

    qb = jnp.concatenate([q1, q2], axis=1).astype(BF16)
    kb = jnp.concatenate([k1, k2], axis=1).astype(BF16)
    qs = jnp.concatenate([q1 * qd, q2 * qd], axis=1).astype(BF16)
    ks = jnp.concatenate([k1 * kd, k2 * kd], axis=1).astype(BF16)

    scores = lax.dot_general(qb, kb, (((1,), (1,)), ((), ())), preferred_element_type=F32) * dmat_ref[...]
    intra = jnp.dot(scores.astype(BF16), v, preferred_element_type=F32)
    state = state_ref[...]
    cross = jnp.dot(qs, state.astype(BF16), preferred_element_type=F32)
    kv = lax.dot_general(ks, v, (((0,), (0,)), ((), ())), preferred_element_type=F32)
    state_ref[...] = state * cd_ref[...] + kv

    o = intra + cross
    mu = jnp.mean(o, axis=-1, keepdims=True)
    oc = o - mu
    var = jnp.mean(oc * oc, axis=-1, keepdims=True)
    on = oc * lax.rsqrt(var + LN_EPS) * gain_ref[...]
    g = g_ref[...].astype(F32)
    o_ref[...] = (g * jax.nn.sigmoid(g) * on).astype(BF16)


def _retention_tables(seq):
    h = jnp.arange(RET_HEADS, dtype=F32)
    log_gamma = jnp.log1p(-jnp.exp2(-5.0 - h))
    r = jnp.arange(SUPER, dtype=F32)
    chunk = np.arange(SUPER) // CHUNK
    allowed = jnp.asarray(chunk[None, :] <= chunk[:, None])
    dist = jnp.abs(r[:, None] - r[None, :])
    dmat = jnp.where(allowed[None], jnp.exp(log_gamma[:, None, None] * dist[None]), 0.0)
    lane_bcast = lambda t: jnp.broadcast_to(t[:, :, None], (RET_HEADS, SUPER, LANES))
    qd = lane_bcast(jnp.exp(log_gamma[:, None] * (r + 1.0)[None, :]))
    kd = lane_bcast(jnp.exp(log_gamma[:, None] * (SUPER - 1.0 - r)[None, :]))
    cd = jnp.broadcast_to(jnp.exp(log_gamma * SUPER)[:, None, None], (RET_HEADS, 1, RET_HEAD_DIM))
    half = RET_HEAD_DIM // 2
    inv = ROPE_BASE ** (-jnp.arange(half, dtype=F32) / half)
    ang = jnp.arange(seq, dtype=F32)[:, None] * inv[None, :]
    return dict(cos=jnp.cos(ang), sin=jnp.sin(ang), dmat=dmat, qd=qd, kd=kd, cd=cd)


def _retention(proj, tabs, gain, batch, seq):
    n_sc = seq // SUPER
    row = lambda b, h, s: b * n_sc + s
    def col(off):
        return pl.BlockSpec((SUPER, RET_HEAD_DIM), lambda b, h, s: (row(b, h, s), off + h))

    head_tab = lambda shape: pl.BlockSpec((None,) + shape, lambda b, h, s: (h, 0, 0))
    return pl.pallas_call(
        _retention_kernel,
        out_shape=jax.ShapeDtypeStruct((batch * seq, RET_WIDTH), BF16),
        grid=(batch, RET_HEADS, n_sc),
        in_specs=[col(0), col(RET_HEADS), col(2 * RET_HEADS), col(3 * RET_HEADS),
                  pl.BlockSpec((SUPER, LANES), lambda b, h, s: (s, 0)),
                  pl.BlockSpec((SUPER, LANES), lambda b, h, s: (s, 0)),
                  head_tab((SUPER, SUPER)), head_tab((SUPER, LANES)), head_tab((SUPER, LANES)),
                  head_tab((1, RET_HEAD_DIM)),
                  pl.BlockSpec((1, RET_HEAD_DIM), lambda b, h, s: (0, h))],
        out_specs=pl.BlockSpec((SUPER, RET_HEAD_DIM), lambda b, h, s: (row(b, h, s), h)),
        scratch_shapes=[pltpu.VMEM((RET_HEAD_DIM, RET_HEAD_DIM), F32)],
        compiler_params=_params(("parallel", "parallel", "arbitrary")),
        name="retention",
    )(proj, proj, proj, proj, tabs["cos"], tabs["sin"], tabs["dmat"], tabs["qd"], tabs["kd"], tabs["cd"],
      gain.reshape(1, RET_WIDTH))


def _band_attention_kernel(q_ref, k0_ref, k1_ref, k2_ref, v0_ref, v1_ref, v2_ref, bias_ref, o_ref):
    sc = pl.program_id(2)
    q = q_ref[...]
    kcat = jnp.concatenate([k0_ref[...], k1_ref[...], k2_ref[...]], axis=0)
    vcat = jnp.concatenate([v0_ref[...], v1_ref[...], v2_ref[...]], axis=0)
    head_of_lane = lax.broadcasted_iota(I32, (1, MXU_DIM), 1) // ATT_HEAD_DIM
    key_block = lax.broadcasted_iota(I32, (1, KEY_SPAN), 1) // SUPER
    in_sequence = (key_block + sc - 2) >= 0
    acc = jnp.zeros((SUPER, MXU_DIM), F32)
    for j in range(ATT_GROUP):
        mine = head_of_lane == j
        qj = jnp.where(mine, q, jnp.zeros_like(q))
        s = lax.dot_general(qj, kcat, (((1,), (1,)), ((), ())), preferred_element_type=F32)
        s = s * (ATT_HEAD_DIM ** -0.5) + bias_ref[j]
        s = jnp.where(in_sequence, s, MASK_VALUE)
        m = jnp.max(s, axis=-1, keepdims=True)
        p = jnp.exp(s - m)
        denom = jnp.sum(p, axis=-1, keepdims=True)
        pv = jnp.dot(p.astype(BF16), vcat, preferred_element_type=F32)
        acc = jnp.where(mine, pv / denom, acc)
    o_ref[...] = acc.astype(BF16)


def _band_bias(rel_table):
    r = np.arange(SUPER)[:, None]
    c = np.arange(KEY_SPAN)[None, :]
    dist = r - c + 2 * SUPER
    rel_idx = np.clip(dist, -(CHUNK - 1), REL_CLIP) + (CHUNK - 1)
    qi, kc = r // CHUNK, c // CHUNK
    in_band = (kc >= qi) & (kc <= qi + N_PREV_CHUNKS)
    bias = jnp.where(jnp.asarray(in_band)[None], rel_table[:, rel_idx].astype(F32), MASK_VALUE)
    return bias.reshape(ATT_HEADS // ATT_GROUP, ATT_GROUP, SUPER, KEY_SPAN)


def _band_attention(proj, bias, batch, seq):
    n_sc = seq // SUPER
    n_grp = ATT_HEADS // ATT_GROUP
    col0 = 4 * RET_WIDTH // MXU_DIM

    def spec(which, back):
        def imap(g, b, s):
            return (b * n_sc + jnp.maximum(s - back, 0), col0 + which * n_grp + g)
        return pl.BlockSpec((SUPER, MXU_DIM), imap)

    return pl.pallas_call(
        _band_attention_kernel,
        out_shape=jax.ShapeDtypeStruct((batch * seq, ATT_WIDTH), BF16),
        grid=(n_grp, batch, n_sc),
        in_specs=[spec(0, 0), spec(1, 2), spec(1, 1), spec(1, 0), spec(2, 2), spec(2, 1), spec(2, 0),
                  pl.BlockSpec((None, ATT_GROUP, SUPER, KEY_SPAN), lambda g, b, s: (g, 0, 0, 0))],
        out_specs=pl.BlockSpec((SUPER, MXU_DIM), lambda g, b, s: (b * n_sc + s, g)),
        compiler_params=_params(("parallel", "parallel", "arbitrary")),
        name="band_attention",
    )(proj, proj, proj, proj, proj, proj, proj, bias)


def _route(logits_t, rb):
    aff = jax.nn.sigmoid(logits_t)
    sel = aff + rb
    rows = [sel[e:e + 1, :] for e in range(N_EXPERTS)]
    best_score, choice = None, None
    for g in range(N_GROUPS):
        r = rows[g * EXPERTS_PER_GROUP:(g + 1) * EXPERTS_PER_GROUP]
        pair = None
        for a in range(EXPERTS_PER_GROUP):
            for b in range(a + 1, EXPERTS_PER_GROUP):
                s = r[a] + r[b]
                pair = s if pair is None else jnp.maximum(pair, s)
        if g == 0:
            best_score, choice = pair, jnp.zeros_like(pair, dtype=I32)
        else:
            better = pair > best_score
            best_score = jnp.where(better, pair, best_score)
            choice = jnp.where(better, g, choice)
    masks = []
    for e in range(N_EXPERTS):
        g, base = e // EXPERTS_PER_GROUP, (e // EXPERTS_PER_GROUP) * EXPERTS_PER_GROUP
        beaten = jnp.zeros_like(choice)
        for o in range(base, base + EXPERTS_PER_GROUP):
            if o == e:
                continue
            ahead = (rows[o] > rows[e]) | ((rows[o] == rows[e]) & (o < e))
            beaten = beaten + ahead.astype(I32)
        masks.append(((choice == g) & (beaten < 2)).astype(F32))
    mask = jnp.concatenate(masks, axis=0)
    top_aff = mask * aff
    total = jnp.sum(top_aff, axis=0, keepdims=True)
    return mask, top_aff / total


def _outproj_kernel(x_ref, ret_ref, att_ref, w_ref, g_ref, b_ref, rw_ref, rb_ref,
                    x1_ref, x1p_ref, mask_ref, comb_ref):
    acc = jnp.dot(ret_ref[...], w_ref[:RET_WIDTH, :], preferred_element_type=F32)
    acc = acc + jnp.dot(att_ref[...], w_ref[RET_WIDTH:, :], preferred_element_type=F32)
    x1 = _layer_norm(DEEPNORM_ALPHA * x_ref[...] + acc, g_ref[...], b_ref[...])
    x1_ref[...] = x1
    x1p_ref[...] = _pack_rows(x1)
    x_hi = x1.astype(BF16)
    x_lo = (x1 - x_hi.astype(F32)).astype(BF16)
    rw = rw_ref[...]
    both = jnp.dot(x_hi, rw, preferred_element_type=F32)
    low = jnp.dot(x_lo, rw, preferred_element_type=F32)
    logits = both[:, :LANES] + both[:, LANES:] + low[:, :LANES]
    logits_t = logits.T[:N_EXPERTS, :]
    mask, comb = _route(logits_t, rb_ref[...])
    mask_ref[...] = mask
    comb_ref[...] = comb


def _outproj_ln_router(x2d, ret, att, w_bf, ln_g, ln_b, rw_split, rb):
    t = x2d.shape[0]
    tm = min(OUTPROJ_TM, t)
    row = lambda w: pl.BlockSpec((tm, w), lambda i: (i, 0))
    const = lambda shape: pl.BlockSpec(shape, lambda i: (0,) * len(shape))
    lanes = pl.BlockSpec((N_EXPERTS, tm), lambda i: (0, i))
    return pl.pallas_call(
        _outproj_kernel,
        out_shape=(jax.ShapeDtypeStruct((t, D_MODEL), F32),
                   jax.ShapeDtypeStruct((t, HALF_ROW), U32),
                   jax.ShapeDtypeStruct((N_EXPERTS, t), F32),
                   jax.ShapeDtypeStruct((N_EXPERTS, t), F32)),
        grid=(t // tm,),
        in_specs=[row(D_MODEL), row(RET_WIDTH), row(ATT_WIDTH), const((D_MODEL, D_MODEL)),
                  const((1, D_MODEL)), const((1, D_MODEL)), const((D_MODEL, MXU_DIM)), const((N_EXPERTS, 1))],
        out_specs=(row(D_MODEL), row(HALF_ROW), lanes, lanes),
        compiler_params=_params(("parallel",)),
        name="outproj_ln_router",
    )(x2d, ret, att, w_bf, ln_g.reshape(1, D_MODEL), ln_b.reshape(1, D_MODEL), rw_split, rb.reshape(N_EXPERTS, 1))


def _split_router(router_w):
    hi = router_w.astype(BF16)
    lo = (router_w - hi.astype(F32)).astype(BF16)
    out = jnp.zeros((D_MODEL, MXU_DIM), BF16)
    return out.at[:, :N_EXPERTS].set(hi).at[:, LANES:LANES + N_EXPERTS].set(lo)


def _rank_kernel(mask_ref, rank_ref, count_ref, carry_ref):
    @pl.when(pl.program_id(0) == 0)
    def _():
        carry_ref[...] = jnp.zeros_like(carry_ref)

    m = mask_ref[...]
    tt = m.shape[1]
    upper = (lax.broadcasted_iota(I32, (tt, tt), 0) <= lax.broadcasted_iota(I32, (tt, tt), 1)).astype(BF16)
    incl = jnp.dot(m.astype(BF16), upper, preferred_element_type=F32)
    carry = carry_ref[...]
    rank_ref[...] = (carry[:, :1] + incl - m).astype(I32)
    carry = carry + jnp.sum(m, axis=1, keepdims=True)
    carry_ref[...] = carry
    count_ref[...] = carry


def _expert_ranks(mask_t):
    t = mask_t.shape[1]
    tt = min(RANK_TT, t)
    return pl.pallas_call(
        _rank_kernel,
        out_shape=(jax.ShapeDtypeStruct((N_EXPERTS, t), I32), jax.ShapeDtypeStruct((N_EXPERTS, LANES), F32)),
        grid=(t // tt,),
        in_specs=[pl.BlockSpec((N_EXPERTS, tt), lambda i: (0, i))],
        out_specs=(pl.BlockSpec((N_EXPERTS, tt), lambda i: (0, i)),
                   pl.BlockSpec((N_EXPERTS, LANES), lambda i: (0, 0))),
        scratch_shapes=[pltpu.VMEM((N_EXPERTS, LANES), F32)],
        compiler_params=_params(("arbitrary",)),
        name="expert_ranks",
    )(mask_t)


def _dispatch_plan(mask_t, rank_t, counts, comb_t, tm):
    t = mask_t.shape[1]
    n_tiles = 2 * t // tm + N_EXPERTS
    cnt = counts[:, 0].astype(I32)
    tiles = (cnt + tm - 1) // tm
    tile_end = jnp.cumsum(tiles)
    off = (tile_end - tiles) * tm
    sel = mask_t > 0.5
    pos_t = off[:, None] + rank_t
    pos_a = jnp.min(jnp.where(sel, pos_t, n_tiles * tm), axis=0)
    pos_b = jnp.max(jnp.where(sel, pos_t, -1), axis=0)
    e_id = jnp.arange(N_EXPERTS, dtype=I32)[:, None]
    e_a = jnp.min(jnp.where(sel, e_id, N_EXPERTS), axis=0)
    w_a = jnp.sum(jnp.where(sel & (e_id == e_a[None, :]), comb_t, 0.0), axis=0)
    w_b = jnp.sum(jnp.where(sel & (e_id != e_a[None, :]), comb_t, 0.0), axis=0)
    n_valid = tile_end[-1]
    tile_blk = jnp.minimum(jnp.arange(n_tiles, dtype=I32), n_valid - 1)
    tile_e = jnp.sum((tile_end[None, :] <= tile_blk[:, None]).astype(I32), axis=1)
    pad_end = (off + tiles * tm).at[N_EXPERTS - 1].set(n_tiles * tm)
    pad_rows = jnp.stack([off + cnt, pad_end]).astype(I32)
    return dict(pos_a=pos_a.astype(I32), pos_b=pos_b.astype(I32), w_a=w_a, w_b=w_b, n_valid=n_valid.astype(I32),
                tile_blk=tile_blk, tile_e=tile_e, pad_rows=pad_rows, n_tiles=n_tiles)


def _dispatch_kernel(pad_ref, pos_ref, x_hbm, xs_hbm, zero_ref, sem):
    i = pl.program_id(0)
    tt = pos_ref.shape[2] // 2

    def row_copy(src, dst):
        return pltpu.make_async_copy(src, dst, sem)

    @pl.when(i == 0)
    def _():
        zero_ref[...] = jnp.zeros_like(zero_ref)
        for e in range(N_EXPERTS):
            lo, hi = pad_ref[0, e], pad_ref[1, e]

            def start(r, c):
                row_copy(zero_ref.at[pl.ds(0, 1)], xs_hbm.at[pl.ds(r, 1)]).start()
                return c

            def wait(r, c):
                row_copy(zero_ref.at[pl.ds(0, 1)], xs_hbm.at[pl.ds(r, 1)]).wait()
                return c

            lax.fori_loop(lo, hi, start, 0)
            lax.fori_loop(lo, hi, wait, 0)

    def start(k, c):
        src = x_hbm.at[pl.ds(i * tt + k, 1)]
        row_copy(src, xs_hbm.at[pl.ds(pos_ref[0, 0, 2 * k], 1)]).start()
        row_copy(src, xs_hbm.at[pl.ds(pos_ref[0, 0, 2 * k + 1], 1)]).start()
        return c

    def wait(k, c):
        src = x_hbm.at[pl.ds(i * tt + k, 1)]
        row_copy(src, xs_hbm.at[pl.ds(pos_ref[0, 0, 2 * k], 1)]).wait()
        row_copy(src, xs_hbm.at[pl.ds(pos_ref[0, 0, 2 * k + 1], 1)]).wait()
        return c

    lax.fori_loop(0, tt, start, 0)
    lax.fori_loop(0, tt, wait, 0)


def _dispatch(x1p, plan, tt):
    t = x1p.shape[0]
    n_rows = plan["n_tiles"] * FFN_TM
    pos = jnp.stack([plan["pos_a"], plan["pos_b"]], axis=1).reshape(t // tt, 1, 2 * tt)
    return pl.pallas_call(
        _dispatch_kernel,
        out_shape=jax.ShapeDtypeStruct((n_rows, HALF_ROW), U32),
        grid_spec=pltpu.PrefetchScalarGridSpec(
            num_scalar_prefetch=1,
            grid=(t // tt,),
            in_specs=[pl.BlockSpec((1, 1, 2 * tt), lambda i, pad: (i, 0, 0), memory_space=pltpu.SMEM),
                      pl.BlockSpec(memory_space=pl.ANY)],
            out_specs=pl.BlockSpec(memory_space=pl.ANY),
            scratch_shapes=[pltpu.VMEM((8, HALF_ROW), U32), pltpu.SemaphoreType.DMA(())]),
        compiler_params=_params(("arbitrary",)),
        name="dispatch",
    )(plan["pad_rows"], pos, x1p)


def _ffn_kernel(blk_ref, exp_ref, nvalid_ref, xs_ref, wg_ref, wu_ref, wd_ref, ys_ref):
    @pl.when(pl.program_id(0) >= nvalid_ref[0])
    def _():
        ys_ref[...] = jnp.zeros_like(ys_ref)

    @pl.when(pl.program_id(0) < nvalid_ref[0])
    def _():
        lo, hi = _unpack_rows(xs_ref[...])
        lo, hi = lo.astype(BF16), hi.astype(BF16)
        gate = jnp.dot(lo, wg_ref[:HALF_ROW, :], preferred_element_type=F32)
        gate = gate + jnp.dot(hi, wg_ref[HALF_ROW:, :], preferred_element_type=F32)
        up = jnp.dot(lo, wu_ref[:HALF_ROW, :], preferred_element_type=F32)
        up = up + jnp.dot(hi, wu_ref[HALF_ROW:, :], preferred_element_type=F32)
        hidden = (gate * jax.nn.sigmoid(gate) * up).astype(BF16)
        ys_ref[...] = _pack_rows(jnp.dot(hidden, wd_ref[...], preferred_element_type=F32))


def _expert_ffn(xs, plan, wg, wu, wd):
    tm = FFN_TM
    n_tiles = plan["n_tiles"]
    rows = pl.BlockSpec((tm, HALF_ROW), lambda i, blk, e, n: (blk[i], 0))
    return pl.pallas_call(
        _ffn_kernel,
        out_shape=jax.ShapeDtypeStruct((n_tiles * tm, HALF_ROW), U32),
        grid_spec=pltpu.PrefetchScalarGridSpec(
            num_scalar_prefetch=3,
            grid=(n_tiles,),
            in_specs=[rows,
                      pl.BlockSpec((None, D_MODEL, D_FF), lambda i, blk, e, n: (e[i], 0, 0)),
                      pl.BlockSpec((None, D_MODEL, D_FF), lambda i, blk, e, n: (e[i], 0, 0)),
                      pl.BlockSpec((None, D_FF, D_MODEL), lambda i, blk, e, n: (e[i], 0, 0))],
            out_specs=pl.BlockSpec((tm, HALF_ROW), lambda i, blk, e, n: (i, 0))),
        compiler_params=_params(("arbitrary",)),
        name="expert_ffn",
    )(plan["tile_blk"], plan["tile_e"], plan["n_valid"].reshape(1), xs, wg, wu, wd)


def _combine_kernel(pos_ref, x1_ref, w_ref, g_ref, b_ref, ys_hbm, x2_ref, buf_ref, sem):
    tt = x1_ref.shape[0]

    def copy(k, slot):
        return pltpu.make_async_copy(ys_hbm.at[pl.ds(pos_ref[0, 0, 2 * k + slot], 1)],
                                     buf_ref.at[slot, pl.ds(k, 1)], sem)

    def start(k, c):
        copy(k, 0).start()
        copy(k, 1).start()
        return c

    def wait(k, c):
        copy(k, 0).wait()
        copy(k, 1).wait()
        return c

    lax.fori_loop(0, tt, start, 0)
    lax.fori_loop(0, tt, wait, 0)
    a_lo, a_hi = _unpack_rows(buf_ref[0])
    b_lo, b_hi = _unpack_rows(buf_ref[1])
    w = w_ref[...]
    w_a, w_b = w[:, 0:1], w[:, 1:2]
    moe = jnp.concatenate([w_a * a_lo + w_b * b_lo, w_a * a_hi + w_b * b_hi], axis=1)
    x2_ref[...] = _layer_norm(DEEPNORM_ALPHA * x1_ref[...] + moe, g_ref[...], b_ref[...])


def _combine_ln(x1, ys, plan, ln_g, ln_b, tt):
    t = x1.shape[0]
    pos = jnp.stack([plan["pos_a"], plan["pos_b"]], axis=1).reshape(t // tt, 1, 2 * tt)
    w = jnp.concatenate([plan["w_a"][:, None], plan["w_b"][:, None], jnp.zeros((t, LANES - 2), F32)], axis=1)
    const = lambda shape: pl.BlockSpec(shape, lambda i: (0,) * len(shape))
    return pl.pallas_call(
        _combine_kernel,
        out_shape=jax.ShapeDtypeStruct((t, D_MODEL), F32),
        grid=(t // tt,),
        in_specs=[pl.BlockSpec((1, 1, 2 * tt), lambda i: (i, 0, 0), memory_space=pltpu.SMEM),
                  pl.BlockSpec((tt, D_MODEL), lambda i: (i, 0)),
                  pl.BlockSpec((tt, LANES), lambda i: (i, 0)),
                  const((1, D_MODEL)), const((1, D_MODEL)),
                  pl.BlockSpec(memory_space=pl.ANY)],
        out_specs=pl.BlockSpec((tt, D_MODEL), lambda i: (i, 0)),
        scratch_shapes=[pltpu.VMEM((2, tt, HALF_ROW), U32), pltpu.SemaphoreType.DMA(())],
        compiler_params=_params(("arbitrary",)),
        name="combine_ln",
    )(pos, x1, w, ln_g.reshape(1, D_MODEL), ln_b.reshape(1, D_MODEL), ys)


def kernel(x, w_in, ret_norm_gain, rel_bias, w_out, ln1_g, ln1_b, router_w, router_b,
           w_gate, w_up, w_down, ln2_g, ln2_b):
    batch, seq, _ = x.shape
    t = batch * seq
    tabs = _retention_tables(seq)
    rw_split = _split_router(router_w)
    h = x.reshape(t, D_MODEL)
    dispatch_tt = min(DISPATCH_TT, t)
    combine_tt = min(COMBINE_TT, t)
    for l in range(DEPTH):
        proj = _inproj(h, w_in[l].astype(BF16))
        ret = _retention(proj, tabs, ret_norm_gain[l], batch, seq)
        att = _band_attention(proj, _band_bias(rel_bias[l]), batch, seq)
        x1, x1p, mask_t, comb_t = _outproj_ln_router(h, ret, att, w_out[l].astype(BF16), ln1_g[l], ln1_b[l],
                                                     rw_split, router_b)
        rank_t, counts = _expert_ranks(mask_t)
        plan = _dispatch_plan(mask_t, rank_t, counts, comb_t, FFN_TM)
        xs = _dispatch(x1p, plan, dispatch_tt)
        ys = _expert_ffn(xs, plan, w_gate[l].astype(BF16), w_up[l].astype(BF16), w_down[l].astype(BF16))
        h = _combine_ln(x1, ys, plan, ln2_g[l], ln2_b[l], combine_tt)
    return h.reshape(batch, seq, D_MODEL)
```

```python
import functools

import jax
import jax.numpy as jnp
import numpy as np
from jax import lax
from jax.experimental import pallas as pl
from jax.experimental.pallas import tpu as pltpu

F32 = jnp.float32
BF16 = jnp.bfloat16
I32 = jnp.int32
U32 = jnp.uint32

D_MODEL = 2048
DEPTH = 4
CHUNK = 64
RET_WIDTH = 1024
RET_HEADS = 4
RET_HEAD_DIM = 256
ATT_WIDTH = 1024
ATT_HEAD_DIM = 64
ATT_HEADS = 16
N_PREV_CHUNKS = 8
REL_CLIP = 256
IN_WIDTH = 4 * RET_WIDTH + 3 * ATT_WIDTH
N_EXPERTS = 16
N_GROUPS = 4
EXPERTS_PER_GROUP = 4
D_FF = 1024
ROPE_BASE = 10000.0
LN_EPS = 1e-5
DEEPNORM_ALPHA = (2 * DEPTH) ** 0.25

LANES = 128
MXU_DIM = 256
VMEM_LIMIT_BYTES = 56 * 1024 * 1024

SUPER = 4 * CHUNK
ATT_GROUP = MXU_DIM // ATT_HEAD_DIM
KEY_SPAN = 3 * SUPER
HALF_ROW = D_MODEL // 2
MASK_VALUE = -1e30
LOG2E = 1.4426950408889634

INPROJ_TM = 1024
INPROJ_TN = 1024
RET_BATCH_PER_STEP = 2
OUTPROJ_TM = 256
RANK_TT = 512
DISPATCH_TT = 512
FFN_TM = 512
COMBINE_TT = 256


def _params(semantics):
    return pltpu.CompilerParams(dimension_semantics=semantics, vmem_limit_bytes=VMEM_LIMIT_BYTES)


def _layer_norm(y, g, b):
    mu = jnp.mean(y, axis=-1, keepdims=True)
    yc = y - mu
    var = jnp.mean(yc * yc, axis=-1, keepdims=True)
    return yc * lax.rsqrt(var + LN_EPS) * g + b


def _pack_rows(y):
    yb = y.astype(BF16).astype(F32)
    lo = lax.bitcast_convert_type(yb[:, :HALF_ROW], U32)
    hi = lax.bitcast_convert_type(yb[:, HALF_ROW:], U32)
    return (lo >> 16) | (hi & jnp.uint32(0xFFFF0000))


def _unpack_rows(w):
    lo = lax.bitcast_convert_type(w << 16, F32)
    hi = lax.bitcast_convert_type(w & jnp.uint32(0xFFFF0000), F32)
    return lo, hi


def _inproj_kernel(x_ref, w_ref, o_ref, xb_ref):
    @pl.when(pl.program_id(1) == 0)
    def _():
        xb_ref[...] = x_ref[...].astype(BF16)

    o_ref[...] = jnp.dot(xb_ref[...], w_ref[...], preferred_element_type=F32).astype(BF16)


def _inproj(x2d, w_bf):
    t = x2d.shape[0]
    tm = min(INPROJ_TM, t)
    return pl.pallas_call(
        _inproj_kernel,
        out_shape=jax.ShapeDtypeStruct((t, IN_WIDTH), BF16),
        grid=(t // tm, IN_WIDTH // INPROJ_TN),
        in_specs=[pl.BlockSpec((tm, D_MODEL), lambda i, j: (i, 0)),
                  pl.BlockSpec((D_MODEL, INPROJ_TN), lambda i, j: (0, j))],
        out_specs=pl.BlockSpec((tm, INPROJ_TN), lambda i, j: (i, j)),
        scratch_shapes=[pltpu.VMEM((tm, D_MODEL), BF16)],
        compiler_params=_params(("parallel", "arbitrary")),
        name="inproj",
    )(x2d, w_bf)


def _rotate(t, cos, sin):
    t1, t2 = t[:, :LANES], t[:, LANES:]
    return t1 * cos - t2 * sin, t1 * sin + t2 * cos


def _retention_kernel(q_ref, k_ref, v_ref, g_ref, cos_ref, sin_ref, dmat_ref, qd_ref, kd_ref, cd_ref,
                      gain_ref, o_ref, state_ref):
    @pl.when(pl.program_id(2) == 0)
    def _():
        state_ref[...] = jnp.zeros_like(state_ref)

    cos, sin = cos_ref[...], sin_ref[...]
    qd, kd = qd_ref[...], kd_ref[...]
    scale = RET_HEAD_DIM ** -0.5
    for i in range(q_ref.shape[0]):
        q1, q2 = _rotate(q_ref[i].astype(F32), cos, sin)
        k1, k2 = _rotate(k_ref[i].astype(F32), cos, sin)
        k1, k2 = k1 * scale, k2 * scale
        v = v_ref[i]

        qb = jnp.concatenate([q1, q2], axis=1).astype(BF16)
        kb = jnp.concatenate([k1, k2], axis=1).astype(BF16)
        qs = jnp.concatenate([q1 * qd, q2 * qd], axis=1).astype(BF16)
        ks = jnp.concatenate([k1 * kd, k2 * kd], axis=1).astype(BF16)

        scores = lax.dot_general(qb, kb, (((1,), (1,)), ((), ())), preferred_element_type=F32) * dmat_ref[...]
        intra = jnp.dot(scores.astype(BF16), v, preferred_element_type=F32)
        state = state_ref[i]
        cross = jnp.dot(qs, state.astype(BF16), preferred_element_type=F32)
        kv = lax.dot_general(ks, v, (((0,), (0,)), ((), ())), preferred_element_type=F32)
        state_ref[i] = state * cd_ref[...] + kv

        o = intra + cross
        mu = jnp.mean(o, axis=-1, keepdims=True)
        oc = o - mu
        var = jnp.mean(oc * oc, axis=-1, keepdims=True)
        on = oc * lax.rsqrt(var + LN_EPS) * gain_ref[...]
        g = g_ref[i].astype(F32)
        o_ref[i] = (g * jax.nn.sigmoid(g) * on).astype(BF16)


def _retention_tables(seq):
    h = jnp.arange(RET_HEADS, dtype=F32)
    log_gamma = jnp.log1p(-jnp.exp2(-5.0 - h))
    r = jnp.arange(SUPER, dtype=F32)
    chunk = np.arange(SUPER) // CHUNK
    allowed = jnp.asarray(chunk[None, :] <= chunk[:, None])
    dist = jnp.abs(r[:, None] - r[None, :])
    dmat = jnp.where(allowed[None], jnp.exp(log_gamma[:, None, None] * dist[None]), 0.0)
    lane_bcast = lambda t: jnp.broadcast_to(t[:, :, None], (RET_HEADS, SUPER, LANES))
    qd = lane_bcast(jnp.exp(log_gamma[:, None] * (r + 1.0)[None, :]))
    kd = lane_bcast(jnp.exp(log_gamma[:, None] * (SUPER - 1.0 - r)[None, :]))
    cd = jnp.broadcast_to(jnp.exp(log_gamma * SUPER)[:, None, None], (RET_HEADS, 1, RET_HEAD_DIM))
    half = RET_HEAD_DIM // 2
    inv = ROPE_BASE ** (-jnp.arange(half, dtype=F32) / half)
    ang = jnp.arange(seq, dtype=F32)[:, None] * inv[None, :]
    return dict(cos=jnp.cos(ang), sin=jnp.sin(ang), dmat=dmat, qd=qd, kd=kd, cd=cd)


def _retention(proj, tabs, gain, batch, seq):
    n_sc = seq // SUPER
    nb = RET_BATCH_PER_STEP if batch % RET_BATCH_PER_STEP == 0 else 1
    proj = proj.reshape(batch, seq, IN_WIDTH)

    def col(off):
        return pl.BlockSpec((nb, SUPER, RET_HEAD_DIM), lambda b, h, s: (b, s, off + h))

    head_tab = lambda shape: pl.BlockSpec((None,) + shape, lambda b, h, s: (h, 0, 0))
    out = pl.pallas_call(
        _retention_kernel,
        out_shape=jax.ShapeDtypeStruct((batch, seq, RET_WIDTH), BF16),
        grid=(batch // nb, RET_HEADS, n_sc),
        in_specs=[col(0), col(RET_HEADS), col(2 * RET_HEADS), col(3 * RET_HEADS),
                  pl.BlockSpec((SUPER, LANES), lambda b, h, s: (s, 0)),
                  pl.BlockSpec((SUPER, LANES), lambda b, h, s: (s, 0)),
                  head_tab((SUPER, SUPER)), head_tab((SUPER, LANES)), head_tab((SUPER, LANES)),
                  head_tab((1, RET_HEAD_DIM)),
                  pl.BlockSpec((1, RET_HEAD_DIM), lambda b, h, s: (0, h))],
        out_specs=pl.BlockSpec((nb, SUPER, RET_HEAD_DIM), lambda b, h, s: (b, s, h)),
        scratch_shapes=[pltpu.VMEM((nb, RET_HEAD_DIM, RET_HEAD_DIM), F32)],
        compiler_params=_params(("parallel", "parallel", "arbitrary")),
        name="retention",
    )(proj, proj, proj, proj, tabs["cos"], tabs["sin"], tabs["dmat"], tabs["qd"], tabs["kd"], tabs["cd"],
      gain.reshape(1, RET_WIDTH))
    return out.reshape(batch * seq, RET_WIDTH)


def _band_attention_kernel(q_ref, k0_ref, k1_ref, k2_ref, v0_ref, v1_ref, v2_ref, bias_ref, o_ref):
    sc = pl.program_id(2)

    def attend(mask_early_blocks):
        q = q_ref[...]
        kt = jnp.concatenate([k0_ref[...].T, k1_ref[...].T, k2_ref[...].T], axis=1)
        vcat = jnp.concatenate([v0_ref[...], v1_ref[...], v2_ref[...]], axis=0)
        head_of_lane = lax.broadcasted_iota(I32, (1, MXU_DIM), 1) // ATT_HEAD_DIM
        if mask_early_blocks:
            key_block = lax.broadcasted_iota(I32, (1, KEY_SPAN), 1) // SUPER
            in_sequence = (key_block + sc - 2) >= 0
        acc = jnp.zeros((SUPER, MXU_DIM), F32)
        for j in range(ATT_GROUP):
            mine = head_of_lane == j
            qj = jnp.where(mine, q, jnp.zeros_like(q))
            s = jnp.dot(qj, kt, preferred_element_type=F32)
            s = s * (ATT_HEAD_DIM ** -0.5 * LOG2E) + bias_ref[j]
            if mask_early_blocks:
                s = jnp.where(in_sequence, s, MASK_VALUE)
            m = jnp.max(s, axis=-1, keepdims=True)
            p = jnp.exp2(s - m)
            denom = jnp.sum(p, axis=-1, keepdims=True)
            pv = jnp.dot(p.astype(BF16), vcat, preferred_element_type=F32)
            acc = jnp.where(mine, pv / denom, acc)
        o_ref[...] = acc.astype(BF16)

    @pl.when(sc < 2)
    def _():
        attend(True)

    @pl.when(sc >= 2)
    def _():
        attend(False)


def _band_bias(rel_table):
    r = np.arange(SUPER)[:, None]
    c = np.arange(KEY_SPAN)[None, :]
    qi, kc = r // CHUNK, c // CHUNK
    in_band = (kc >= qi) & (kc <= qi + N_PREV_CHUNKS)
    period = SUPER + KEY_SPAN - 1
    n = np.arange(period)
    c_minus_r = np.where(n < KEY_SPAN, n, n - period)
    rel_idx = np.clip(2 * SUPER - c_minus_r, -(CHUNK - 1), REL_CLIP) + (CHUNK - 1)
    per_dist = rel_table[:, rel_idx].astype(F32)
    flat = jnp.tile(per_dist, (1, SUPER))[:, :SUPER * (period - 1)]
    toeplitz = flat.reshape(ATT_HEADS, SUPER, period - 1)[:, :, :KEY_SPAN]
    bias = jnp.where(jnp.asarray(in_band)[None], toeplitz * LOG2E, MASK_VALUE)
    return bias.reshape(ATT_HEADS // ATT_GROUP, ATT_GROUP, SUPER, KEY_SPAN)


def _band_attention(proj, bias, batch, seq):
    n_sc = seq // SUPER
    n_grp = ATT_HEADS // ATT_GROUP
    col0 = 4 * RET_WIDTH // MXU_DIM

    def spec(which, back):
        def imap(g, b, s):
            return (b * n_sc + jnp.maximum(s - back, 0), col0 + which * n_grp + g)
        return pl.BlockSpec((SUPER, MXU_DIM), imap)

    return pl.pallas_call(
        _band_attention_kernel,
        out_shape=jax.ShapeDtypeStruct((batch * seq, ATT_WIDTH), BF16),
        grid=(n_grp, batch, n_sc),
        in_specs=[spec(0, 0), spec(1, 2), spec(1, 1), spec(1, 0), spec(2, 2), spec(2, 1), spec(2, 0),
                  pl.BlockSpec((None, ATT_GROUP, SUPER, KEY_SPAN), lambda g, b, s: (g, 0, 0, 0))],
        out_specs=pl.BlockSpec((SUPER, MXU_DIM), lambda g, b, s: (b * n_sc + s, g)),
        compiler_params=_params(("parallel", "parallel", "arbitrary")),
        name="band_attention",
    )(proj, proj, proj, proj, proj, proj, proj, bias)


def _route(logits_t, rb):
    aff = jax.nn.sigmoid(logits_t)
    sel = aff + rb
    rows = [sel[e:e + 1, :] for e in range(N_EXPERTS)]
    best_score, choice = None, None
    for g in range(N_GROUPS):
        r = rows[g * EXPERTS_PER_GROUP:(g + 1) * EXPERTS_PER_GROUP]
        pair = None
        for a in range(EXPERTS_PER_GROUP):
            for b in range(a + 1, EXPERTS_PER_GROUP):
                s = r[a] + r[b]
                pair = s if pair is None else jnp.maximum(pair, s)
        if g == 0:
            best_score, choice = pair, jnp.zeros_like(pair, dtype=I32)
        else:
            better = pair > best_score
            best_score = jnp.where(better, pair, best_score)
            choice = jnp.where(better, g, choice)
    masks = []
    for e in range(N_EXPERTS):
        g, base = e // EXPERTS_PER_GROUP, (e // EXPERTS_PER_GROUP) * EXPERTS_PER_GROUP
        beaten = jnp.zeros_like(choice)
        for o in range(base, base + EXPERTS_PER_GROUP):
            if o == e:
                continue
            ahead = (rows[o] > rows[e]) | ((rows[o] == rows[e]) & (o < e))
            beaten = beaten + ahead.astype(I32)
        masks.append(((choice == g) & (beaten < 2)).astype(F32))
    mask = jnp.concatenate(masks, axis=0)
    top_aff = mask * aff
    total = jnp.sum(top_aff, axis=0, keepdims=True)
    return mask, top_aff / total


def _outproj_kernel(x_ref, ret_ref, att_ref, w_ref, g_ref, b_ref, rw_ref, rb_ref,
                    x1_ref, x1p_ref, mask_ref, comb_ref):
    acc = jnp.dot(ret_ref[...], w_ref[:RET_WIDTH, :], preferred_element_type=F32)
    acc = acc + jnp.dot(att_ref[...], w_ref[RET_WIDTH:, :], preferred_element_type=F32)
    x1 = _layer_norm(DEEPNORM_ALPHA * x_ref[...] + acc, g_ref[...], b_ref[...])
    x1_ref[...] = x1
    x1p_ref[...] = _pack_rows(x1)
    x_hi = x1.astype(BF16)
    x_lo = (x1 - x_hi.astype(F32)).astype(BF16)
    rw = rw_ref[...]
    both = jnp.dot(x_hi, rw, preferred_element_type=F32)
    low = jnp.dot(x_lo, rw, preferred_element_type=F32)
    logits = both[:, :LANES] + both[:, LANES:] + low[:, :LANES]
    logits_t = logits.T[:N_EXPERTS, :]
    mask, comb = _route(logits_t, rb_ref[...])
    mask_ref[...] = mask
    comb_ref[...] = comb


def _outproj_ln_router(x2d, ret, att, w_bf, ln_g, ln_b, rw_split, rb):
    t = x2d.shape[0]
    tm = min(OUTPROJ_TM, t)
    row = lambda w: pl.BlockSpec((tm, w), lambda i: (i, 0))
    const = lambda shape: pl.BlockSpec(shape, lambda i: (0,) * len(shape))
    lanes = pl.BlockSpec((N_EXPERTS, tm), lambda i: (0, i))
    return pl.pallas_call(
        _outproj_kernel,
        out_shape=(jax.ShapeDtypeStruct((t, D_MODEL), F32),
                   jax.ShapeDtypeStruct((t, HALF_ROW), U32),
                   jax.ShapeDtypeStruct((N_EXPERTS, t), F32),
                   jax.ShapeDtypeStruct((N_EXPERTS, t), F32)),
        grid=(t // tm,),
        in_specs=[row(D_MODEL), row(RET_WIDTH), row(ATT_WIDTH), const((D_MODEL, D_MODEL)),
                  const((1, D_MODEL)), const((1, D_MODEL)), const((D_MODEL, MXU_DIM)), const((N_EXPERTS, 1))],
        out_specs=(row(D_MODEL), row(HALF_ROW), lanes, lanes),
        compiler_params=_params(("parallel",)),
        name="outproj_ln_router",
    )(x2d, ret, att, w_bf, ln_g.reshape(1, D_MODEL), ln_b.reshape(1, D_MODEL), rw_split, rb.reshape(N_EXPERTS, 1))


def _split_router(router_w):
    hi = router_w.astype(BF16)
    lo = (router_w - hi.astype(F32)).astype(BF16)
    out = jnp.zeros((D_MODEL, MXU_DIM), BF16)
    return out.at[:, :N_EXPERTS].set(hi).at[:, LANES:LANES + N_EXPERTS].set(lo)


def _rank_kernel(mask_ref, rank_ref, count_ref, carry_ref):
    @pl.when(pl.program_id(0) == 0)
    def _():
        carry_ref[...] = jnp.zeros_like(carry_ref)

    m = mask_ref[...]
    tt = m.shape[1]
    upper = (lax.broadcasted_iota(I32, (tt, tt), 0) <= lax.broadcasted_iota(I32, (tt, tt), 1)).astype(BF16)
    incl = jnp.dot(m.astype(BF16), upper, preferred_element_type=F32)
    carry = carry_ref[...]
    rank_ref[...] = (carry[:, :1] + incl - m).astype(I32)
    carry = carry + jnp.sum(m, axis=1, keepdims=True)
    carry_ref[...] = carry
    count_ref[...] = carry


def _expert_ranks(mask_t):
    t = mask_t.shape[1]
    tt = min(RANK_TT, t)
    return pl.pallas_call(
        _rank_kernel,
        out_shape=(jax.ShapeDtypeStruct((N_EXPERTS, t), I32), jax.ShapeDtypeStruct((N_EXPERTS, LANES), F32)),
        grid=(t // tt,),
        in_specs=[pl.BlockSpec((N_EXPERTS, tt), lambda i: (0, i))],
        out_specs=(pl.BlockSpec((N_EXPERTS, tt), lambda i: (0, i)),
                   pl.BlockSpec((N_EXPERTS, LANES), lambda i: (0, 0))),
        scratch_shapes=[pltpu.VMEM((N_EXPERTS, LANES), F32)],
        compiler_params=_params(("arbitrary",)),
        name="expert_ranks",
    )(mask_t)


def _dispatch_plan(mask_t, rank_t, counts, comb_t, tm):
    t = mask_t.shape[1]
    n_tiles = 2 * t // tm + N_EXPERTS
    cnt = counts[:, 0].astype(I32)
    tiles = (cnt + tm - 1) // tm
    tile_end = jnp.cumsum(tiles)
    off = (tile_end - tiles) * tm
    sel = mask_t > 0.5
    pos_t = off[:, None] + rank_t
    pos_a = jnp.min(jnp.where(sel, pos_t, n_tiles * tm), axis=0)
    pos_b = jnp.max(jnp.where(sel, pos_t, -1), axis=0)
    e_id = jnp.arange(N_EXPERTS, dtype=I32)[:, None]
    e_a = jnp.min(jnp.where(sel, e_id, N_EXPERTS), axis=0)
    w_a = jnp.sum(jnp.where(sel & (e_id == e_a[None, :]), comb_t, 0.0), axis=0)
    w_b = jnp.sum(jnp.where(sel & (e_id != e_a[None, :]), comb_t, 0.0), axis=0)
    n_valid = tile_end[-1]
    tile_blk = jnp.minimum(jnp.arange(n_tiles, dtype=I32), n_valid - 1)
    tile_e = jnp.sum((tile_end[None, :] <= tile_blk[:, None]).astype(I32), axis=1)
    pad_end = (off + tiles * tm).at[N_EXPERTS - 1].set(n_tiles * tm)
    pad_rows = jnp.stack([off + cnt, pad_end]).astype(I32)
    return dict(pos_a=pos_a.astype(I32), pos_b=pos_b.astype(I32), w_a=w_a, w_b=w_b, n_valid=n_valid.astype(I32),
                tile_blk=tile_blk, tile_e=tile_e, pad_rows=pad_rows, n_tiles=n_tiles)


def _dispatch_kernel(pad_ref, pos_ref, x_ref, xs_hbm, zero_ref, sem):
    i = pl.program_id(0)
    tt = x_ref.shape[0]

    def row_copy(src, dst):
        return pltpu.make_async_copy(src, dst, sem)

    @pl.when(i == 0)
    def _():
        zero_ref[...] = jnp.zeros_like(zero_ref)
        for e in range(N_EXPERTS):
            lo, hi = pad_ref[0, e], pad_ref[1, e]

            def start(r, c):
                row_copy(zero_ref.at[pl.ds(0, 1)], xs_hbm.at[pl.ds(r, 1)]).start()
                return c

            def wait(r, c):
                row_copy(zero_ref.at[pl.ds(0, 1)], xs_hbm.at[pl.ds(r, 1)]).wait()
                return c

            lax.fori_loop(lo, hi, start, 0)
            lax.fori_loop(lo, hi, wait, 0)

    def start(k, c):
        src = x_ref.at[pl.ds(k, 1)]
        row_copy(src, xs_hbm.at[pl.ds(pos_ref[0, 0, 2 * k], 1)]).start()
        row_copy(src, xs_hbm.at[pl.ds(pos_ref[0, 0, 2 * k + 1], 1)]).start()
        return c

    def wait(k, c):
        src = x_ref.at[pl.ds(k, 1)]
        row_copy(src, xs_hbm.at[pl.ds(pos_ref[0, 0, 2 * k], 1)]).wait()
        row_copy(src, xs_hbm.at[pl.ds(pos_ref[0, 0, 2 * k + 1], 1)]).wait()
        return c

    lax.fori_loop(0, tt, start, 0, unroll=8)
    lax.fori_loop(0, tt, wait, 0, unroll=8)


def _dispatch(x1p, plan, tt):
    t = x1p.shape[0]
    n_rows = plan["n_tiles"] * FFN_TM
    pos = jnp.stack([plan["pos_a"], plan["pos_b"]], axis=1).reshape(t // tt, 1, 2 * tt)
    return pl.pallas_call(
        _dispatch_kernel,
        out_shape=jax.ShapeDtypeStruct((n_rows, HALF_ROW), U32),
        grid_spec=pltpu.PrefetchScalarGridSpec(
            num_scalar_prefetch=1,
            grid=(t // tt,),
            in_specs=[pl.BlockSpec((1, 1, 2 * tt), lambda i, pad: (i, 0, 0), memory_space=pltpu.SMEM),
                      pl.BlockSpec((tt, HALF_ROW), lambda i, pad: (i, 0))],
            out_specs=pl.BlockSpec(memory_space=pl.ANY),
            scratch_shapes=[pltpu.VMEM((8, HALF_ROW), U32), pltpu.SemaphoreType.DMA(())]),
        compiler_params=_params(("arbitrary",)),
        name="dispatch",
    )(plan["pad_rows"], pos, x1p)


def _ffn_kernel(blk_ref, exp_ref, nvalid_ref, xs_ref, wg_ref, wu_ref, wd_ref, ys_ref):
    @pl.when(pl.program_id(0) >= nvalid_ref[0])
    def _():
        ys_ref[...] = jnp.zeros_like(ys_ref)

    @pl.when(pl.program_id(0) < nvalid_ref[0])
    def _():
        lo, hi = _unpack_rows(xs_ref[...])
        lo, hi = lo.astype(BF16), hi.astype(BF16)
        gate = jnp.dot(lo, wg_ref[:HALF_ROW, :], preferred_element_type=F32)
        gate = gate + jnp.dot(hi, wg_ref[HALF_ROW:, :], preferred_element_type=F32)
        up = jnp.dot(lo, wu_ref[:HALF_ROW, :], preferred_element_type=F32)
        up = up + jnp.dot(hi, wu_ref[HALF_ROW:, :], preferred_element_type=F32)
        hidden = (gate * jax.nn.sigmoid(gate) * up).astype(BF16)
        ys_ref[...] = _pack_rows(jnp.dot(hidden, wd_ref[...], preferred_element_type=F32))


def _expert_ffn(xs, plan, wg, wu, wd):
    tm = FFN_TM
    n_tiles = plan["n_tiles"]
    rows = pl.BlockSpec((tm, HALF_ROW), lambda i, blk, e, n: (blk[i], 0))
    return pl.pallas_call(
        _ffn_kernel,
        out_shape=jax.ShapeDtypeStruct((n_tiles * tm, HALF_ROW), U32),
        grid_spec=pltpu.PrefetchScalarGridSpec(
            num_scalar_prefetch=3,
            grid=(n_tiles,),
            in_specs=[rows,
                      pl.BlockSpec((None, D_MODEL, D_FF), lambda i, blk, e, n: (e[i], 0, 0)),
                      pl.BlockSpec((None, D_MODEL, D_FF), lambda i, blk, e, n: (e[i], 0, 0)),
                      pl.BlockSpec((None, D_FF, D_MODEL), lambda i, blk, e, n: (e[i], 0, 0))],
            out_specs=pl.BlockSpec((tm, HALF_ROW), lambda i, blk, e, n: (i, 0))),
        compiler_params=_params(("arbitrary",)),
        name="expert_ffn",
    )(plan["tile_blk"], plan["tile_e"], plan["n_valid"].reshape(1), xs, wg, wu, wd)


def _combine_kernel(pos_ref, next_pos_ref, x1_ref, w_ref, g_ref, b_ref, ys_hbm, x2_ref, buf_ref, sem):
    i = pl.program_id(0)
    tt = x1_ref.shape[0]
    slot = i % 2

    def copy(p_ref, k, which, s):
        return pltpu.make_async_copy(ys_hbm.at[pl.ds(p_ref[0, 0, 2 * k + which], 1)],
                                     buf_ref.at[s, which, pl.ds(k, 1)], sem.at[s])

    def gather(p_ref, s):
        def start(k, c):
            copy(p_ref, k, 0, s).start()
            copy(p_ref, k, 1, s).start()
            return c
        lax.fori_loop(0, tt, start, 0, unroll=8)

    @pl.when(i == 0)
    def _():
        gather(pos_ref, slot)

    @pl.when(i + 1 < pl.num_programs(0))
    def _():
        gather(next_pos_ref, 1 - slot)

    def wait(k, c):
        copy(pos_ref, k, 0, slot).wait()
        copy(pos_ref, k, 1, slot).wait()
        return c

    lax.fori_loop(0, tt, wait, 0, unroll=8)
    a_lo, a_hi = _unpack_rows(buf_ref[slot, 0])
    b_lo, b_hi = _unpack_rows(buf_ref[slot, 1])
    w = w_ref[...]
    w_a, w_b = w[:, 0:1], w[:, 1:2]
    moe = jnp.concatenate([w_a * a_lo + w_b * b_lo, w_a * a_hi + w_b * b_hi], axis=1)
    x2_ref[...] = _layer_norm(DEEPNORM_ALPHA * x1_ref[...] + moe, g_ref[...], b_ref[...])


def _combine_ln(x1, ys, plan, ln_g, ln_b, tt):
    t = x1.shape[0]
    pos = jnp.stack([plan["pos_a"], plan["pos_b"]], axis=1).reshape(t // tt, 1, 2 * tt)
    w = jnp.concatenate([plan["w_a"][:, None], plan["w_b"][:, None], jnp.zeros((t, LANES - 2), F32)], axis=1)
    const = lambda shape: pl.BlockSpec(shape, lambda i: (0,) * len(shape))
    n_steps = t // tt
    return pl.pallas_call(
        _combine_kernel,
        out_shape=jax.ShapeDtypeStruct((t, D_MODEL), F32),
        grid=(n_steps,),
        in_specs=[pl.BlockSpec((1, 1, 2 * tt), lambda i: (i, 0, 0), memory_space=pltpu.SMEM),
                  pl.BlockSpec((1, 1, 2 * tt), lambda i: (jnp.minimum(i + 1, n_steps - 1), 0, 0),
                               memory_space=pltpu.SMEM),
                  pl.BlockSpec((tt, D_MODEL), lambda i: (i, 0)),
                  pl.BlockSpec((tt, LANES), lambda i: (i, 0)),
                  const((1, D_MODEL)), const((1, D_MODEL)),
                  pl.BlockSpec(memory_space=pl.ANY)],
        out_specs=pl.BlockSpec((tt, D_MODEL), lambda i: (i, 0)),
        scratch_shapes=[pltpu.VMEM((2, 2, tt, HALF_ROW), U32), pltpu.SemaphoreType.DMA((2,))],
        compiler_params=_params(("arbitrary",)),
        name="combine_ln",
    )(pos, pos, x1, w, ln_g.reshape(1, D_MODEL), ln_b.reshape(1, D_MODEL), ys)


def kernel(x, w_in, ret_norm_gain, rel_bias, w_out, ln1_g, ln1_b, router_w, router_b,
           w_gate, w_up, w_down, ln2_g, ln2_b):
    batch, seq, _ = x.shape
    t = batch * seq
    tabs = _retention_tables(seq)
    rw_split = _split_router(router_w)
    h = x.reshape(t, D_MODEL)
    dispatch_tt = min(DISPATCH_TT, t)
    combine_tt = min(COMBINE_TT, t)
    for l in range(DEPTH):
        proj = _inproj(h, w_in[l].astype(BF16))
        ret = _retention(proj, tabs, ret_norm_gain[l], batch, seq)
        att = _band_attention(proj, _band_bias(rel_bias[l]), batch, seq)
        x1, x1p, mask_t, comb_t = _outproj_ln_router(h, ret, att, w_out[l].astype(BF16), ln1_g[l], ln1_b[l],
                                                     rw_split, router_b)
        rank_t, counts = _expert_ranks(mask_t)
        plan = _dispatch_plan(mask_t, rank_t, counts, comb_t, FFN_TM)
        xs = _dispatch(x1p, plan, dispatch_tt)
        ys = _expert_ffn(xs, plan, w_gate[l].astype(BF16), w_up[l].astype(BF16), w_down[l].astype(BF16))
        h = _combine_ln(x1, ys, plan, ln2_g[l], ln2_b[l], combine_tt)
    return h.reshape(batch, seq, D_MODEL)
```

```python
import functools

import jax
import jax.numpy as jnp
import numpy as np
from jax import lax
from jax.experimental import pallas as pl
from jax.experimental.pallas import tpu as pltpu

F32 = jnp.float32
BF16 = jnp.bfloat16
I32 = jnp.int32
U32 = jnp.uint32

D_MODEL = 2048
DEPTH = 4
CHUNK = 64
RET_WIDTH = 1024
RET_HEADS = 4
RET_HEAD_DIM = 256
ATT_WIDTH = 1024
ATT_HEAD_DIM = 64
ATT_HEADS = 16
N_PREV_CHUNKS = 8
REL_CLIP = 256
IN_WIDTH = 4 * RET_WIDTH + 3 * ATT_WIDTH
N_EXPERTS = 16
N_GROUPS = 4
EXPERTS_PER_GROUP = 4
D_FF = 1024
ROPE_BASE = 10000.0
LN_EPS = 1e-5
DEEPNORM_ALPHA = (2 * DEPTH) ** 0.25

LANES = 128
SUBLANES = 8
MXU_DIM = 256
VMEM_LIMIT_BYTES = 56 * 1024 * 1024

SUPER = 4 * CHUNK
ATT_GROUP = MXU_DIM // ATT_HEAD_DIM
KEY_SPAN = 3 * SUPER
HALF_ROW = D_MODEL // 2
MASK_VALUE = -1e30
LOG2E = 1.4426950408889634

INPROJ_TM = 1024
INPROJ_TN = 1024
RET_BATCH_PER_STEP = 4
ATT_GROUPS_PER_STEP = 4
OUTPROJ_TM = 256
RANK_TT = 512
DISPATCH_TT = 512
FFN_TM = 512
COMBINE_TT = 256


def _params(semantics):
    return pltpu.CompilerParams(dimension_semantics=semantics, vmem_limit_bytes=VMEM_LIMIT_BYTES)


def _layer_norm(y, g, b):
    mu = jnp.mean(y, axis=-1, keepdims=True)
    yc = y - mu
    var = jnp.mean(yc * yc, axis=-1, keepdims=True)
    return yc * lax.rsqrt(var + LN_EPS) * g + b


def _pack_rows(y):
    yb = y.astype(BF16).astype(F32)
    lo = lax.bitcast_convert_type(yb[:, :HALF_ROW], U32)
    hi = lax.bitcast_convert_type(yb[:, HALF_ROW:], U32)
    return (lo >> 16) | (hi & jnp.uint32(0xFFFF0000))


def _unpack_rows(w):
    lo = lax.bitcast_convert_type(w << 16, F32)
    hi = lax.bitcast_convert_type(w & jnp.uint32(0xFFFF0000), F32)
    return lo, hi


def _inproj_kernel(x_ref, w_ref, o_ref, xb_ref):
    @pl.when(pl.program_id(1) == 0)
    def _():
        xb_ref[...] = x_ref[...].astype(BF16)

    o_ref[...] = jnp.dot(xb_ref[...], w_ref[...], preferred_element_type=F32).astype(BF16)


def _inproj(x2d, w_bf, layer):
    t = x2d.shape[0]
    tm = min(INPROJ_TM, t)
    return pl.pallas_call(
        _inproj_kernel,
        out_shape=jax.ShapeDtypeStruct((t, IN_WIDTH), BF16),
        grid=(t // tm, IN_WIDTH // INPROJ_TN),
        in_specs=[pl.BlockSpec((tm, D_MODEL), lambda i, j: (i, 0)),
                  pl.BlockSpec((None, D_MODEL, INPROJ_TN), lambda i, j: (layer, 0, j))],
        out_specs=pl.BlockSpec((tm, INPROJ_TN), lambda i, j: (i, j)),
        scratch_shapes=[pltpu.VMEM((tm, D_MODEL), BF16)],
        compiler_params=_params(("parallel", "arbitrary")),
        name="inproj",
    )(x2d, w_bf)


def _rotate(t, cos, sin):
    t1, t2 = t[:, :LANES], t[:, LANES:]
    return t1 * cos - t2 * sin, t1 * sin + t2 * cos


def _retention_kernel(q_ref, k_ref, v_ref, g_ref, cos_ref, sin_ref, dmat_ref, qd_ref, kd_ref, cd_ref,
                      gain_ref, o_ref, state_ref):
    @pl.when(pl.program_id(2) == 0)
    def _():
        state_ref[...] = jnp.zeros_like(state_ref)

    cos, sin = cos_ref[...], sin_ref[...]
    qd, kd = qd_ref[...], kd_ref[...]
    scale = RET_HEAD_DIM ** -0.5
    for i in range(q_ref.shape[0]):
        q1, q2 = _rotate(q_ref[i].astype(F32), cos, sin)
        k1, k2 = _rotate(k_ref[i].astype(F32), cos, sin)
        k1, k2 = k1 * scale, k2 * scale
        v = v_ref[i]

        qb = jnp.concatenate([q1, q2], axis=1).astype(BF16)
        kb = jnp.concatenate([k1, k2], axis=1).astype(BF16)
        qs = jnp.concatenate([q1 * qd, q2 * qd], axis=1).astype(BF16)
        ks = jnp.concatenate([k1 * kd, k2 * kd], axis=1).astype(BF16)

        scores = lax.dot_general(qb, kb, (((1,), (1,)), ((), ())), preferred_element_type=F32) * dmat_ref[...]
        intra = jnp.dot(scores.astype(BF16), v, preferred_element_type=F32)
        state = state_ref[i]
        cross = jnp.dot(qs, state.astype(BF16), preferred_element_type=F32)
        kv = lax.dot_general(ks, v, (((0,), (0,)), ((), ())), preferred_element_type=F32)
        state_ref[i] = state * cd_ref[...] + kv

        o = intra + cross
        mu = jnp.mean(o, axis=-1, keepdims=True)
        oc = o - mu
        var = jnp.mean(oc * oc, axis=-1, keepdims=True)
        on = oc * lax.rsqrt(var + LN_EPS) * gain_ref[...]
        g = g_ref[i].astype(F32)
        o_ref[i] = (g * jax.nn.sigmoid(g) * on).astype(BF16)


def _retention_tables(seq):
    h = jnp.arange(RET_HEADS, dtype=F32)
    log_gamma = jnp.log1p(-jnp.exp2(-5.0 - h))
    r = jnp.arange(SUPER, dtype=F32)
    chunk = np.arange(SUPER) // CHUNK
    allowed = jnp.asarray(chunk[None, :] <= chunk[:, None])
    dist = jnp.abs(r[:, None] - r[None, :])
    dmat = jnp.where(allowed[None], jnp.exp(log_gamma[:, None, None] * dist[None]), 0.0)
    lane_bcast = lambda t: jnp.broadcast_to(t[:, :, None], (RET_HEADS, SUPER, LANES))
    qd = lane_bcast(jnp.exp(log_gamma[:, None] * (r + 1.0)[None, :]))
    kd = lane_bcast(jnp.exp(log_gamma[:, None] * (SUPER - 1.0 - r)[None, :]))
    cd = jnp.broadcast_to(jnp.exp(log_gamma * SUPER)[:, None, None], (RET_HEADS, 1, RET_HEAD_DIM))
    half = RET_HEAD_DIM // 2
    inv = ROPE_BASE ** (-jnp.arange(half, dtype=F32) / half)
    ang = jnp.arange(seq, dtype=F32)[:, None] * inv[None, :]
    return dict(cos=jnp.cos(ang), sin=jnp.sin(ang), dmat=dmat, qd=qd, kd=kd, cd=cd)


def _retention(proj, tabs, gain, batch, seq):
    n_sc = seq // SUPER
    nb = RET_BATCH_PER_STEP if batch % RET_BATCH_PER_STEP == 0 else 1
    proj = proj.reshape(batch, seq, IN_WIDTH)

    def col(off):
        return pl.BlockSpec((nb, SUPER, RET_HEAD_DIM), lambda b, h, s: (b, s, off + h))

    head_tab = lambda shape: pl.BlockSpec((None,) + shape, lambda b, h, s: (h, 0, 0))
    out = pl.pallas_call(
        _retention_kernel,
        out_shape=jax.ShapeDtypeStruct((batch, seq, RET_WIDTH), BF16),
        grid=(batch // nb, RET_HEADS, n_sc),
        in_specs=[col(0), col(RET_HEADS), col(2 * RET_HEADS), col(3 * RET_HEADS),
                  pl.BlockSpec((SUPER, LANES), lambda b, h, s: (s, 0)),
                  pl.BlockSpec((SUPER, LANES), lambda b, h, s: (s, 0)),
                  head_tab((SUPER, SUPER)), head_tab((SUPER, LANES)), head_tab((SUPER, LANES)),
                  head_tab((1, RET_HEAD_DIM)),
                  pl.BlockSpec((1, RET_HEAD_DIM), lambda b, h, s: (0, h))],
        out_specs=pl.BlockSpec((nb, SUPER, RET_HEAD_DIM), lambda b, h, s: (b, s, h)),
        scratch_shapes=[pltpu.VMEM((nb, RET_HEAD_DIM, RET_HEAD_DIM), F32)],
        compiler_params=_params(("parallel", "parallel", "arbitrary")),
        name="retention",
    )(proj, proj, proj, proj, tabs["cos"], tabs["sin"], tabs["dmat"], tabs["qd"], tabs["kd"], tabs["cd"],
      gain.reshape(1, RET_WIDTH))
    return out.reshape(batch * seq, RET_WIDTH)


def _band_attention_kernel(q_ref, k0_ref, k1_ref, k2_ref, v0_ref, v1_ref, v2_ref, bias_ref, o_ref):
    sc = pl.program_id(2)

    def attend(mask_early_blocks):
        head_of_lane = lax.broadcasted_iota(I32, (1, MXU_DIM), 1) // ATT_HEAD_DIM
        if mask_early_blocks:
            key_block = lax.broadcasted_iota(I32, (1, KEY_SPAN), 1) // SUPER
            in_sequence = (key_block + sc - 2) >= 0
        for gi in range(q_ref.shape[1] // MXU_DIM):
            cols = slice(gi * MXU_DIM, (gi + 1) * MXU_DIM)
            q = q_ref[:, cols]
            kt = jnp.concatenate([k0_ref[:, cols].T, k1_ref[:, cols].T, k2_ref[:, cols].T], axis=1)
            vcat = jnp.concatenate([v0_ref[:, cols], v1_ref[:, cols], v2_ref[:, cols]], axis=0)
            acc = jnp.zeros((SUPER, MXU_DIM), F32)
            for j in range(ATT_GROUP):
                mine = head_of_lane == j
                qj = jnp.where(mine, q, jnp.zeros_like(q))
                s = jnp.dot(qj, kt, preferred_element_type=F32)
                s = s * (ATT_HEAD_DIM ** -0.5 * LOG2E) + bias_ref[gi * ATT_GROUP + j]
                if mask_early_blocks:
                    s = jnp.where(in_sequence, s, MASK_VALUE)
                m = jnp.max(s, axis=-1, keepdims=True)
                p = jnp.exp2(s - m)
                denom = jnp.sum(p, axis=-1, keepdims=True)
                pv = jnp.dot(p.astype(BF16), vcat, preferred_element_type=F32)
                acc = jnp.where(mine, pv / denom, acc)
            o_ref[:, cols] = acc.astype(BF16)

    @pl.when(sc < 2)
    def _():
        attend(True)

    @pl.when(sc >= 2)
    def _():
        attend(False)


def _band_bias(rel_table):
    r = np.arange(SUPER)[:, None]
    c = np.arange(KEY_SPAN)[None, :]
    qi, kc = r // CHUNK, c // CHUNK
    in_band = (kc >= qi) & (kc <= qi + N_PREV_CHUNKS)
    period = SUPER + KEY_SPAN - 1
    n = np.arange(period)
    c_minus_r = np.where(n < KEY_SPAN, n, n - period)
    rel_idx = np.clip(2 * SUPER - c_minus_r, -(CHUNK - 1), REL_CLIP) + (CHUNK - 1)
    per_dist = rel_table[:, rel_idx].astype(F32)
    flat = jnp.tile(per_dist, (1, SUPER))[:, :SUPER * (period - 1)]
    toeplitz = flat.reshape(ATT_HEADS, SUPER, period - 1)[:, :, :KEY_SPAN]
    bias = jnp.where(jnp.asarray(in_band)[None], toeplitz * LOG2E, MASK_VALUE)
    heads_per_step = ATT_GROUP * ATT_GROUPS_PER_STEP
    return bias.reshape(ATT_HEADS // heads_per_step, heads_per_step, SUPER, KEY_SPAN)


def _band_attention(proj, bias, batch, seq):
    n_sc = seq // SUPER
    width = MXU_DIM * ATT_GROUPS_PER_STEP
    n_blk = ATT_WIDTH // width
    col0 = 4 * RET_WIDTH // width

    def spec(which, back):
        def imap(g, b, s):
            return (b * n_sc + jnp.maximum(s - back, 0), col0 + which * n_blk + g)
        return pl.BlockSpec((SUPER, width), imap)

    return pl.pallas_call(
        _band_attention_kernel,
        out_shape=jax.ShapeDtypeStruct((batch * seq, ATT_WIDTH), BF16),
        grid=(n_blk, batch, n_sc),
        in_specs=[spec(0, 0), spec(1, 2), spec(1, 1), spec(1, 0), spec(2, 2), spec(2, 1), spec(2, 0),
                  pl.BlockSpec((None, ATT_GROUP * ATT_GROUPS_PER_STEP, SUPER, KEY_SPAN), lambda g, b, s: (g, 0, 0, 0))],
        out_specs=pl.BlockSpec((SUPER, width), lambda g, b, s: (b * n_sc + s, g)),
        compiler_params=_params(("parallel", "parallel", "arbitrary")),
        name="band_attention",
    )(proj, proj, proj, proj, proj, proj, proj, bias)


def _route(logits_t, rb):
    aff = jax.nn.sigmoid(logits_t)
    sel = aff + rb
    rows = [sel[e:e + 1, :] for e in range(N_EXPERTS)]
    best_score, choice = None, None
    for g in range(N_GROUPS):
        r = rows[g * EXPERTS_PER_GROUP:(g + 1) * EXPERTS_PER_GROUP]
        pair = None
        for a in range(EXPERTS_PER_GROUP):
            for b in range(a + 1, EXPERTS_PER_GROUP):
                s = r[a] + r[b]
                pair = s if pair is None else jnp.maximum(pair, s)
        if g == 0:
            best_score, choice = pair, jnp.zeros_like(pair, dtype=I32)
        else:
            better = pair > best_score
            best_score = jnp.where(better, pair, best_score)
            choice = jnp.where(better, g, choice)
    masks = []
    for e in range(N_EXPERTS):
        g, base = e // EXPERTS_PER_GROUP, (e // EXPERTS_PER_GROUP) * EXPERTS_PER_GROUP
        beaten = jnp.zeros_like(choice)
        for o in range(base, base + EXPERTS_PER_GROUP):
            if o == e:
                continue
            ahead = (rows[o] > rows[e]) | ((rows[o] == rows[e]) & (o < e))
            beaten = beaten + ahead.astype(I32)
        masks.append(((choice == g) & (beaten < 2)).astype(F32))
    mask = jnp.concatenate(masks, axis=0)
    top_aff = mask * aff
    total = jnp.sum(top_aff, axis=0, keepdims=True)
    return mask, top_aff / total


def _outproj_kernel(x_ref, ret_ref, att_ref, w_ref, g_ref, b_ref, rw_ref, rb_ref,
                    x1_ref, x1p_ref, mask_ref, comb_ref):
    acc = jnp.dot(ret_ref[...], w_ref[:RET_WIDTH, :], preferred_element_type=F32)
    acc = acc + jnp.dot(att_ref[...], w_ref[RET_WIDTH:, :], preferred_element_type=F32)
    x1 = _layer_norm(DEEPNORM_ALPHA * x_ref[...] + acc, g_ref[...], b_ref[...])
    x1_ref[...] = x1
    x1p_ref[...] = _pack_rows(x1)
    x_hi = x1.astype(BF16)
    x_lo = (x1 - x_hi.astype(F32)).astype(BF16)
    rw = rw_ref[...]
    both = jnp.dot(x_hi, rw, preferred_element_type=F32)
    low = jnp.dot(x_lo, rw, preferred_element_type=F32)
    logits = both[:, :LANES] + both[:, LANES:] + low[:, :LANES]
    logits_t = logits.T[:N_EXPERTS, :]
    mask, comb = _route(logits_t, rb_ref[...])
    mask_ref[...] = mask
    comb_ref[...] = comb


def _outproj_ln_router(x2d, ret, att, w_bf, layer, ln_g, ln_b, rw_split, rb):
    t = x2d.shape[0]
    tm = min(OUTPROJ_TM, t)
    row = lambda w: pl.BlockSpec((tm, w), lambda i: (i, 0))
    const = lambda shape: pl.BlockSpec(shape, lambda i: (0,) * len(shape))
    w_spec = pl.BlockSpec((None, D_MODEL, D_MODEL), lambda i: (layer, 0, 0))
    lanes = pl.BlockSpec((N_EXPERTS, tm), lambda i: (0, i))
    return pl.pallas_call(
        _outproj_kernel,
        out_shape=(jax.ShapeDtypeStruct((t, D_MODEL), F32),
                   jax.ShapeDtypeStruct((t, HALF_ROW), U32),
                   jax.ShapeDtypeStruct((N_EXPERTS, t), F32),
                   jax.ShapeDtypeStruct((N_EXPERTS, t), F32)),
        grid=(t // tm,),
        in_specs=[row(D_MODEL), row(RET_WIDTH), row(ATT_WIDTH), w_spec,
                  const((1, D_MODEL)), const((1, D_MODEL)), const((D_MODEL, MXU_DIM)), const((N_EXPERTS, 1))],
        out_specs=(row(D_MODEL), row(HALF_ROW), lanes, lanes),
        compiler_params=_params(("parallel",)),
        name="outproj_ln_router",
    )(x2d, ret, att, w_bf, ln_g.reshape(1, D_MODEL), ln_b.reshape(1, D_MODEL), rw_split, rb.reshape(N_EXPERTS, 1))


def _split_router(router_w):
    hi = router_w.astype(BF16)
    lo = (router_w - hi.astype(F32)).astype(BF16)
    out = jnp.zeros((D_MODEL, MXU_DIM), BF16)
    return out.at[:, :N_EXPERTS].set(hi).at[:, LANES:LANES + N_EXPERTS].set(lo)


def _rank_kernel(mask_ref, rank_ref, count_ref, carry_ref):
    @pl.when(pl.program_id(0) == 0)
    def _():
        carry_ref[...] = jnp.zeros_like(carry_ref)

    m = mask_ref[...]
    tt = m.shape[1]
    upper = (lax.broadcasted_iota(I32, (tt, tt), 0) <= lax.broadcasted_iota(I32, (tt, tt), 1)).astype(BF16)
    incl = jnp.dot(m.astype(BF16), upper, preferred_element_type=F32)
    carry = carry_ref[...]
    rank_ref[...] = (carry[:, :1] + incl - m).astype(I32)
    carry = carry + jnp.sum(m, axis=1, keepdims=True)
    carry_ref[...] = carry
    count_ref[...] = carry


def _expert_ranks(mask_t):
    t = mask_t.shape[1]
    tt = min(RANK_TT, t)
    return pl.pallas_call(
        _rank_kernel,
        out_shape=(jax.ShapeDtypeStruct((N_EXPERTS, t), I32), jax.ShapeDtypeStruct((N_EXPERTS, LANES), F32)),
        grid=(t // tt,),
        in_specs=[pl.BlockSpec((N_EXPERTS, tt), lambda i: (0, i))],
        out_specs=(pl.BlockSpec((N_EXPERTS, tt), lambda i: (0, i)),
                   pl.BlockSpec((N_EXPERTS, LANES), lambda i: (0, 0))),
        scratch_shapes=[pltpu.VMEM((N_EXPERTS, LANES), F32)],
        compiler_params=_params(("arbitrary",)),
        name="expert_ranks",
    )(mask_t)


def _dispatch_plan(mask_t, rank_t, counts, comb_t, tm):
    t = mask_t.shape[1]
    n_tiles = 2 * t // tm + N_EXPERTS
    cnt = counts[:, 0].astype(I32)
    tiles = (cnt + tm - 1) // tm
    tile_end = jnp.cumsum(tiles)
    off = (tile_end - tiles) * tm
    sel = mask_t > 0.5
    pos_t = off[:, None] + rank_t
    pos_a = jnp.min(jnp.where(sel, pos_t, n_tiles * tm), axis=0)
    pos_b = jnp.max(jnp.where(sel, pos_t, -1), axis=0)
    e_id = jnp.arange(N_EXPERTS, dtype=I32)[:, None]
    e_a = jnp.min(jnp.where(sel, e_id, N_EXPERTS), axis=0)
    w_a = jnp.sum(jnp.where(sel & (e_id == e_a[None, :]), comb_t, 0.0), axis=0)
    w_b = jnp.sum(jnp.where(sel & (e_id != e_a[None, :]), comb_t, 0.0), axis=0)
    n_valid = tile_end[-1]
    tile_blk = jnp.minimum(jnp.arange(n_tiles, dtype=I32), n_valid - 1)
    tile_e = jnp.sum((tile_end[None, :] <= tile_blk[:, None]).astype(I32), axis=1)
    pad_end = (off + tiles * tm).at[N_EXPERTS - 1].set(n_tiles * tm)
    pad_rows = jnp.stack([off + cnt, pad_end]).astype(I32)
    return dict(pos_a=pos_a.astype(I32), pos_b=pos_b.astype(I32), w_a=w_a, w_b=w_b, n_valid=n_valid.astype(I32),
                tile_blk=tile_blk, tile_e=tile_e, pad_rows=pad_rows, n_tiles=n_tiles)


def _dispatch_kernel(pad_ref, pos_ref, x_ref, xs_hbm, zero_ref, sem):
    i = pl.program_id(0)
    tt = x_ref.shape[0]

    def row_copy(src, dst):
        return pltpu.make_async_copy(src, dst, sem)

    @pl.when(i == 0)
    def _():
        zero_ref[...] = jnp.zeros_like(zero_ref)
        for e in range(N_EXPERTS):
            lo, hi = pad_ref[0, e], pad_ref[1, e]

            def start(r, c):
                row_copy(zero_ref.at[pl.ds(0, 1)], xs_hbm.at[pl.ds(r, 1)]).start()
                return c

            def wait(r, c):
                row_copy(zero_ref.at[pl.ds(0, 1)], xs_hbm.at[pl.ds(r, 1)]).wait()
                return c

            lax.fori_loop(lo, hi, start, 0)
            lax.fori_loop(lo, hi, wait, 0)

    def copies(j, u):
        src = x_ref.at[pl.ds(pl.multiple_of(j * SUBLANES, SUBLANES), SUBLANES)].at[pl.ds(u, 1)]
        k = j * SUBLANES + u
        return (row_copy(src, xs_hbm.at[pl.ds(pos_ref[0, 0, 2 * k], 1)]),
                row_copy(src, xs_hbm.at[pl.ds(pos_ref[0, 0, 2 * k + 1], 1)]))

    def start(j, c):
        for u in range(SUBLANES):
            a, b = copies(j, u)
            a.start(priority=0)
            b.start(priority=1)
        return c

    def wait(j, c):
        for u in range(SUBLANES):
            a, b = copies(j, u)
            a.wait()
            b.wait()
        return c

    lax.fori_loop(0, tt // SUBLANES, start, 0)
    lax.fori_loop(0, tt // SUBLANES, wait, 0)


def _dispatch(x1p, plan, tt):
    t = x1p.shape[0]
    n_rows = plan["n_tiles"] * FFN_TM
    pos = jnp.stack([plan["pos_a"], plan["pos_b"]], axis=1).reshape(t // tt, 1, 2 * tt)
    return pl.pallas_call(
        _dispatch_kernel,
        out_shape=jax.ShapeDtypeStruct((n_rows, HALF_ROW), U32),
        grid_spec=pltpu.PrefetchScalarGridSpec(
            num_scalar_prefetch=1,
            grid=(t // tt,),
            in_specs=[pl.BlockSpec((1, 1, 2 * tt), lambda i, pad: (i, 0, 0), memory_space=pltpu.SMEM),
                      pl.BlockSpec((tt, HALF_ROW), lambda i, pad: (i, 0))],
            out_specs=pl.BlockSpec(memory_space=pl.ANY),
            scratch_shapes=[pltpu.VMEM((SUBLANES, HALF_ROW), U32), pltpu.SemaphoreType.DMA(())]),
        compiler_params=_params(("arbitrary",)),
        name="dispatch",
    )(plan["pad_rows"], pos, x1p)


def _ffn_kernel(blk_ref, exp_ref, nvalid_ref, xs_ref, wg_ref, wu_ref, wd_ref, ys_ref):
    @pl.when(pl.program_id(0) >= nvalid_ref[0])
    def _():
        ys_ref[...] = jnp.zeros_like(ys_ref)

    @pl.when(pl.program_id(0) < nvalid_ref[0])
    def _():
        lo, hi = _unpack_rows(xs_ref[...])
        lo, hi = lo.astype(BF16), hi.astype(BF16)
        gate = jnp.dot(lo, wg_ref[:HALF_ROW, :], preferred_element_type=F32)
        gate = gate + jnp.dot(hi, wg_ref[HALF_ROW:, :], preferred_element_type=F32)
        up = jnp.dot(lo, wu_ref[:HALF_ROW, :], preferred_element_type=F32)
        up = up + jnp.dot(hi, wu_ref[HALF_ROW:, :], preferred_element_type=F32)
        hidden = (gate * jax.nn.sigmoid(gate) * up).astype(BF16)
        ys_ref[...] = _pack_rows(jnp.dot(hidden, wd_ref[...], preferred_element_type=F32))


def _expert_ffn(xs, plan, wg, wu, wd, layer):
    tm = FFN_TM
    n_tiles = plan["n_tiles"]
    rows = pl.BlockSpec((tm, HALF_ROW), lambda i, blk, e, n: (blk[i], 0))
    w_spec = lambda k, n: pl.BlockSpec((None, None, k, n), lambda i, blk, e, nv: (layer, e[i], 0, 0))
    return pl.pallas_call(
        _ffn_kernel,
        out_shape=jax.ShapeDtypeStruct((n_tiles * tm, HALF_ROW), U32),
        grid_spec=pltpu.PrefetchScalarGridSpec(
            num_scalar_prefetch=3,
            grid=(n_tiles,),
            in_specs=[rows, w_spec(D_MODEL, D_FF), w_spec(D_MODEL, D_FF), w_spec(D_FF, D_MODEL)],
            out_specs=pl.BlockSpec((tm, HALF_ROW), lambda i, blk, e, n: (i, 0))),
        compiler_params=_params(("arbitrary",)),
        name="expert_ffn",
    )(plan["tile_blk"], plan["tile_e"], plan["n_valid"].reshape(1), xs, wg, wu, wd)


def _combine_kernel(pos_ref, next_pos_ref, x1_ref, w_ref, g_ref, b_ref, ys_hbm, x2_ref, buf_ref, sem):
    i = pl.program_id(0)
    tt = x1_ref.shape[0]
    slot = i % 2

    def copy(p_ref, j, u, which, s):
        dst = buf_ref.at[s, which, pl.ds(pl.multiple_of(j * SUBLANES, SUBLANES), SUBLANES)].at[pl.ds(u, 1)]
        k = j * SUBLANES + u
        return pltpu.make_async_copy(ys_hbm.at[pl.ds(p_ref[0, 0, 2 * k + which], 1)], dst, sem.at[s])

    def gather(p_ref, s):
        def start(j, c):
            for u in range(SUBLANES):
                copy(p_ref, j, u, 0, s).start(priority=0)
                copy(p_ref, j, u, 1, s).start(priority=1)
            return c
        lax.fori_loop(0, tt // SUBLANES, start, 0)

    @pl.when(i == 0)
    def _():
        gather(pos_ref, slot)

    @pl.when(i + 1 < pl.num_programs(0))
    def _():
        gather(next_pos_ref, 1 - slot)

    def wait(j, c):
        for u in range(SUBLANES):
            copy(pos_ref, j, u, 0, slot).wait()
            copy(pos_ref, j, u, 1, slot).wait()
        return c

    lax.fori_loop(0, tt // SUBLANES, wait, 0)
    a_lo, a_hi = _unpack_rows(buf_ref[slot, 0])
    b_lo, b_hi = _unpack_rows(buf_ref[slot, 1])
    w = w_ref[...]
    w_a, w_b = w[:, 0:1], w[:, 1:2]
    moe = jnp.concatenate([w_a * a_lo + w_b * b_lo, w_a * a_hi + w_b * b_hi], axis=1)
    x2_ref[...] = _layer_norm(DEEPNORM_ALPHA * x1_ref[...] + moe, g_ref[...], b_ref[...])


def _combine_ln(x1, ys, plan, ln_g, ln_b, tt):
    t = x1.shape[0]
    pos = jnp.stack([plan["pos_a"], plan["pos_b"]], axis=1).reshape(t // tt, 1, 2 * tt)
    w = jnp.concatenate([plan["w_a"][:, None], plan["w_b"][:, None], jnp.zeros((t, LANES - 2), F32)], axis=1)
    const = lambda shape: pl.BlockSpec(shape, lambda i: (0,) * len(shape))
    n_steps = t // tt
    return pl.pallas_call(
        _combine_kernel,
        out_shape=jax.ShapeDtypeStruct((t, D_MODEL), F32),
        grid=(n_steps,),
        in_specs=[pl.BlockSpec((1, 1, 2 * tt), lambda i: (i, 0, 0), memory_space=pltpu.SMEM),
                  pl.BlockSpec((1, 1, 2 * tt), lambda i: (jnp.minimum(i + 1, n_steps - 1), 0, 0),
                               memory_space=pltpu.SMEM),
                  pl.BlockSpec((tt, D_MODEL), lambda i: (i, 0)),
                  pl.BlockSpec((tt, LANES), lambda i: (i, 0)),
                  const((1, D_MODEL)), const((1, D_MODEL)),
                  pl.BlockSpec(memory_space=pl.ANY)],
        out_specs=pl.BlockSpec((tt, D_MODEL), lambda i: (i, 0)),
        scratch_shapes=[pltpu.VMEM((2, 2, tt, HALF_ROW), U32), pltpu.SemaphoreType.DMA((2,))],
        compiler_params=_params(("arbitrary",)),
        name="combine_ln",
    )(pos, pos, x1, w, ln_g.reshape(1, D_MODEL), ln_b.reshape(1, D_MODEL), ys)


def kernel(x, w_in, ret_norm_gain, rel_bias, w_out, ln1_g, ln1_b, router_w, router_b,
           w_gate, w_up, w_down, ln2_g, ln2_b):
    batch, seq, _ = x.shape
    t = batch * seq
    tabs = _retention_tables(seq)
    rw_split = _split_router(router_w)
    h = x.reshape(t, D_MODEL)
    dispatch_tt = min(DISPATCH_TT, t)
    combine_tt = min(COMBINE_TT, t)
    w_in_bf, w_out_bf = w_in.astype(BF16), w_out.astype(BF16)
    wg_bf, wu_bf, wd_bf = w_gate.astype(BF16), w_up.astype(BF16), w_down.astype(BF16)
    for l in range(DEPTH):
        proj = _inproj(h, w_in_bf, l)
        ret = _retention(proj, tabs, ret_norm_gain[l], batch, seq)
        att = _band_attention(proj, _band_bias(rel_bias[l]), batch, seq)
        x1, x1p, mask_t, comb_t = _outproj_ln_router(h, ret, att, w_out_bf, l, ln1_g[l], ln1_b[l],
                                                     rw_split, router_b)
        rank_t, counts = _expert_ranks(mask_t)
        plan = _dispatch_plan(mask_t, rank_t, counts, comb_t, FFN_TM)
        xs = _dispatch(x1p, plan, dispatch_tt)
        ys = _expert_ffn(xs, plan, wg_bf, wu_bf, wd_bf, l)
        h = _combine_ln(x1, ys, plan, ln2_g[l], ln2_b[l], combine_tt)
    return h.reshape(batch, seq, D_MODEL)
```

```python
import functools

import jax
import jax.numpy as jnp
import numpy as np
from jax import lax
from jax.experimental import pallas as pl
from jax.experimental.pallas import tpu as pltpu

F32 = jnp.float32
BF16 = jnp.bfloat16
I32 = jnp.int32
U32 = jnp.uint32

D_MODEL = 2048
DEPTH = 4
CHUNK = 64
RET_WIDTH = 1024
RET_HEADS = 4
RET_HEAD_DIM = 256
ATT_WIDTH = 1024
ATT_HEAD_DIM = 64
ATT_HEADS = 16
N_PREV_CHUNKS = 8
REL_CLIP = 256
IN_WIDTH = 4 * RET_WIDTH + 3 * ATT_WIDTH
N_EXPERTS = 16
N_GROUPS = 4
EXPERTS_PER_GROUP = 4
D_FF = 1024
ROPE_BASE = 10000.0
LN_EPS = 1e-5
DEEPNORM_ALPHA = (2 * DEPTH) ** 0.25

LANES = 128
SUBLANES = 8
MXU_DIM = 256
VMEM_LIMIT_BYTES = 56 * 1024 * 1024

SUPER = 4 * CHUNK
ATT_GROUP = MXU_DIM // ATT_HEAD_DIM
KEY_SPAN = 3 * SUPER
HALF_ROW = D_MODEL // 2
MASK_VALUE = -1e30
LOG2E = 1.4426950408889634

INPROJ_TM = 1024
INPROJ_TN = 1024
RET_BATCH_PER_STEP = 4
ATT_GROUPS_PER_STEP = 4
OUTPROJ_TM = 256
RANK_TT = 512
ROW_MOVE_TT = 512
FFN_TM = 512


def _params(semantics):
    return pltpu.CompilerParams(dimension_semantics=semantics, vmem_limit_bytes=VMEM_LIMIT_BYTES)


def _layer_norm(y, g, b):
    mu = jnp.mean(y, axis=-1, keepdims=True)
    yc = y - mu
    var = jnp.mean(yc * yc, axis=-1, keepdims=True)
    return yc * lax.rsqrt(var + LN_EPS) * g + b


def _pack_rows(y):
    yb = y.astype(BF16).astype(F32)
    lo = lax.bitcast_convert_type(yb[:, :HALF_ROW], U32)
    hi = lax.bitcast_convert_type(yb[:, HALF_ROW:], U32)
    return (lo >> 16) | (hi & jnp.uint32(0xFFFF0000))


def _unpack_rows(w):
    lo = lax.bitcast_convert_type(w << 16, F32)
    hi = lax.bitcast_convert_type(w & jnp.uint32(0xFFFF0000), F32)
    return lo, hi


def _inproj_kernel(x_ref, w_ref, o_ref, xb_ref):
    @pl.when(pl.program_id(1) == 0)
    def _():
        xb_ref[...] = x_ref[...].astype(BF16)

    o_ref[...] = jnp.dot(xb_ref[...], w_ref[...], preferred_element_type=F32).astype(BF16)


def _inproj(x2d, w_bf, layer):
    t = x2d.shape[0]
    tm = min(INPROJ_TM, t)
    return pl.pallas_call(
        _inproj_kernel,
        out_shape=jax.ShapeDtypeStruct((t, IN_WIDTH), BF16),
        grid=(t // tm, IN_WIDTH // INPROJ_TN),
        in_specs=[pl.BlockSpec((tm, D_MODEL), lambda i, j: (i, 0)),
                  pl.BlockSpec((None, D_MODEL, INPROJ_TN), lambda i, j: (layer, 0, j))],
        out_specs=pl.BlockSpec((tm, INPROJ_TN), lambda i, j: (i, j)),
        scratch_shapes=[pltpu.VMEM((tm, D_MODEL), BF16)],
        compiler_params=_params(("parallel", "arbitrary")),
        name="inproj",
    )(x2d, w_bf)


def _rotate(t, cos, sin):
    t1, t2 = t[:, :LANES], t[:, LANES:]
    return t1 * cos - t2 * sin, t1 * sin + t2 * cos


def _retention_kernel(q_ref, k_ref, v_ref, g_ref, cos_ref, sin_ref, dmat_ref, qd_ref, kd_ref, cd_ref,
                      gain_ref, o_ref, state_ref):
    @pl.when(pl.program_id(2) == 0)
    def _():
        state_ref[...] = jnp.zeros_like(state_ref)

    cos, sin = cos_ref[...], sin_ref[...]
    qd, kd = qd_ref[...], kd_ref[...]
    scale = RET_HEAD_DIM ** -0.5
    for i in range(q_ref.shape[0]):
        q1, q2 = _rotate(q_ref[i].astype(F32), cos, sin)
        k1, k2 = _rotate(k_ref[i].astype(F32), cos, sin)
        k1, k2 = k1 * scale, k2 * scale
        v = v_ref[i]

        qb = jnp.concatenate([q1, q2], axis=1).astype(BF16)
        kb = jnp.concatenate([k1, k2], axis=1).astype(BF16)
        qs = jnp.concatenate([q1 * qd, q2 * qd], axis=1).astype(BF16)
        ks = jnp.concatenate([k1 * kd, k2 * kd], axis=1).astype(BF16)

        scores = lax.dot_general(qb, kb, (((1,), (1,)), ((), ())), preferred_element_type=F32) * dmat_ref[...]
        intra = jnp.dot(scores.astype(BF16), v, preferred_element_type=F32)
        state = state_ref[i]
        cross = jnp.dot(qs, state.astype(BF16), preferred_element_type=F32)
        kv = lax.dot_general(ks, v, (((0,), (0,)), ((), ())), preferred_element_type=F32)
        state_ref[i] = state * cd_ref[...] + kv

        o = intra + cross
        mu = jnp.mean(o, axis=-1, keepdims=True)
        oc = o - mu
        var = jnp.mean(oc * oc, axis=-1, keepdims=True)
        on = oc * lax.rsqrt(var + LN_EPS) * gain_ref[...]
        g = g_ref[i].astype(F32)
        o_ref[i] = (g * jax.nn.sigmoid(g) * on).astype(BF16)


def _retention_tables(seq):
    h = jnp.arange(RET_HEADS, dtype=F32)
    log_gamma = jnp.log1p(-jnp.exp2(-5.0 - h))
    r = jnp.arange(SUPER, dtype=F32)
    chunk = np.arange(SUPER) // CHUNK
    allowed = jnp.asarray(chunk[None, :] <= chunk[:, None])
    dist = jnp.abs(r[:, None] - r[None, :])
    dmat = jnp.where(allowed[None], jnp.exp(log_gamma[:, None, None] * dist[None]), 0.0)
    lane_bcast = lambda t: jnp.broadcast_to(t[:, :, None], (RET_HEADS, SUPER, LANES))
    qd = lane_bcast(jnp.exp(log_gamma[:, None] * (r + 1.0)[None, :]))
    kd = lane_bcast(jnp.exp(log_gamma[:, None] * (SUPER - 1.0 - r)[None, :]))
    cd = jnp.broadcast_to(jnp.exp(log_gamma * SUPER)[:, None, None], (RET_HEADS, 1, RET_HEAD_DIM))
    half = RET_HEAD_DIM // 2
    inv = ROPE_BASE ** (-jnp.arange(half, dtype=F32) / half)
    ang = jnp.arange(seq, dtype=F32)[:, None] * inv[None, :]
    return dict(cos=jnp.cos(ang), sin=jnp.sin(ang), dmat=dmat, qd=qd, kd=kd, cd=cd)


def _retention(proj, tabs, gain, batch, seq):
    n_sc = seq // SUPER
    nb = RET_BATCH_PER_STEP if batch % RET_BATCH_PER_STEP == 0 else 1
    proj = proj.reshape(batch, seq, IN_WIDTH)

    def col(off):
        return pl.BlockSpec((nb, SUPER, RET_HEAD_DIM), lambda b, h, s: (b, s, off + h))

    head_tab = lambda shape: pl.BlockSpec((None,) + shape, lambda b, h, s: (h, 0, 0))
    out = pl.pallas_call(
        _retention_kernel,
        out_shape=jax.ShapeDtypeStruct((batch, seq, RET_WIDTH), BF16),
        grid=(batch // nb, RET_HEADS, n_sc),
        in_specs=[col(0), col(RET_HEADS), col(2 * RET_HEADS), col(3 * RET_HEADS),
                  pl.BlockSpec((SUPER, LANES), lambda b, h, s: (s, 0)),
                  pl.BlockSpec((SUPER, LANES), lambda b, h, s: (s, 0)),
                  head_tab((SUPER, SUPER)), head_tab((SUPER, LANES)), head_tab((SUPER, LANES)),
                  head_tab((1, RET_HEAD_DIM)),
                  pl.BlockSpec((1, RET_HEAD_DIM), lambda b, h, s: (0, h))],
        out_specs=pl.BlockSpec((nb, SUPER, RET_HEAD_DIM), lambda b, h, s: (b, s, h)),
        scratch_shapes=[pltpu.VMEM((nb, RET_HEAD_DIM, RET_HEAD_DIM), F32)],
        compiler_params=_params(("parallel", "parallel", "arbitrary")),
        name="retention",
    )(proj, proj, proj, proj, tabs["cos"], tabs["sin"], tabs["dmat"], tabs["qd"], tabs["kd"], tabs["cd"],
      gain.reshape(1, RET_WIDTH))
    return out.reshape(batch * seq, RET_WIDTH)


def _band_attention_kernel(q_ref, k0_ref, k1_ref, k2_ref, v0_ref, v1_ref, v2_ref, bias_ref, o_ref):
    sc = pl.program_id(2)

    def attend(mask_early_blocks):
        head_of_lane = lax.broadcasted_iota(I32, (1, MXU_DIM), 1) // ATT_HEAD_DIM
        if mask_early_blocks:
            key_block = lax.broadcasted_iota(I32, (1, KEY_SPAN), 1) // SUPER
            in_sequence = (key_block + sc - 2) >= 0
        for gi in range(q_ref.shape[1] // MXU_DIM):
            cols = slice(gi * MXU_DIM, (gi + 1) * MXU_DIM)
            q = q_ref[:, cols]
            kt = jnp.concatenate([k0_ref[:, cols].T, k1_ref[:, cols].T, k2_ref[:, cols].T], axis=1)
            vcat = jnp.concatenate([v0_ref[:, cols], v1_ref[:, cols], v2_ref[:, cols]], axis=0)
            acc = jnp.zeros((SUPER, MXU_DIM), F32)
            for j in range(ATT_GROUP):
                mine = head_of_lane == j
                qj = jnp.where(mine, q, jnp.zeros_like(q))
                s = jnp.dot(qj, kt, preferred_element_type=F32)
                s = s * (ATT_HEAD_DIM ** -0.5 * LOG2E) + bias_ref[gi * ATT_GROUP + j]
                if mask_early_blocks:
                    s = jnp.where(in_sequence, s, MASK_VALUE)
                m = jnp.max(s, axis=-1, keepdims=True)
                p = jnp.exp2(s - m)
                denom = jnp.sum(p, axis=-1, keepdims=True)
                pv = jnp.dot(p.astype(BF16), vcat, preferred_element_type=F32)
                acc = jnp.where(mine, pv / denom, acc)
            o_ref[:, cols] = acc.astype(BF16)

    @pl.when(sc < 2)
    def _():
        attend(True)

    @pl.when(sc >= 2)
    def _():
        attend(False)


def _band_bias(rel_table):
    r = np.arange(SUPER)[:, None]
    c = np.arange(KEY_SPAN)[None, :]
    qi, kc = r // CHUNK, c // CHUNK
    in_band = (kc >= qi) & (kc <= qi + N_PREV_CHUNKS)
    period = SUPER + KEY_SPAN - 1
    n = np.arange(period)
    c_minus_r = np.where(n < KEY_SPAN, n, n - period)
    rel_idx = np.clip(2 * SUPER - c_minus_r, -(CHUNK - 1), REL_CLIP) + (CHUNK - 1)
    per_dist = rel_table[:, rel_idx].astype(F32)
    flat = jnp.tile(per_dist, (1, SUPER))[:, :SUPER * (period - 1)]
    toeplitz = flat.reshape(ATT_HEADS, SUPER, period - 1)[:, :, :KEY_SPAN]
    bias = jnp.where(jnp.asarray(in_band)[None], toeplitz * LOG2E, MASK_VALUE)
    heads_per_step = ATT_GROUP * ATT_GROUPS_PER_STEP
    return bias.reshape(ATT_HEADS // heads_per_step, heads_per_step, SUPER, KEY_SPAN)


def _band_attention(proj, bias, batch, seq):
    n_sc = seq // SUPER
    width = MXU_DIM * ATT_GROUPS_PER_STEP
    n_blk = ATT_WIDTH // width
    col0 = 4 * RET_WIDTH // width

    def spec(which, back):
        def imap(g, b, s):
            return (b * n_sc + jnp.maximum(s - back, 0), col0 + which * n_blk + g)
        return pl.BlockSpec((SUPER, width), imap)

    return pl.pallas_call(
        _band_attention_kernel,
        out_shape=jax.ShapeDtypeStruct((batch * seq, ATT_WIDTH), BF16),
        grid=(n_blk, batch, n_sc),
        in_specs=[spec(0, 0), spec(1, 2), spec(1, 1), spec(1, 0), spec(2, 2), spec(2, 1), spec(2, 0),
                  pl.BlockSpec((None, ATT_GROUP * ATT_GROUPS_PER_STEP, SUPER, KEY_SPAN), lambda g, b, s: (g, 0, 0, 0))],
        out_specs=pl.BlockSpec((SUPER, width), lambda g, b, s: (b * n_sc + s, g)),
        compiler_params=_params(("parallel", "parallel", "arbitrary")),
        name="band_attention",
    )(proj, proj, proj, proj, proj, proj, proj, bias)


def _route(logits_t, rb):
    aff = jax.nn.sigmoid(logits_t)
    sel = aff + rb
    rows = [sel[e:e + 1, :] for e in range(N_EXPERTS)]
    best_score, choice = None, None
    for g in range(N_GROUPS):
        r = rows[g * EXPERTS_PER_GROUP:(g + 1) * EXPERTS_PER_GROUP]
        pair = None
        for a in range(EXPERTS_PER_GROUP):
            for b in range(a + 1, EXPERTS_PER_GROUP):
                s = r[a] + r[b]
                pair = s if pair is None else jnp.maximum(pair, s)
        if g == 0:
            best_score, choice = pair, jnp.zeros_like(pair, dtype=I32)
        else:
            better = pair > best_score
            best_score = jnp.where(better, pair, best_score)
            choice = jnp.where(better, g, choice)
    masks = []
    for e in range(N_EXPERTS):
        g, base = e // EXPERTS_PER_GROUP, (e // EXPERTS_PER_GROUP) * EXPERTS_PER_GROUP
        beaten = jnp.zeros_like(choice)
        for o in range(base, base + EXPERTS_PER_GROUP):
            if o == e:
                continue
            ahead = (rows[o] > rows[e]) | ((rows[o] == rows[e]) & (o < e))
            beaten = beaten + ahead.astype(I32)
        masks.append(((choice == g) & (beaten < 2)).astype(F32))
    mask = jnp.concatenate(masks, axis=0)
    top_aff = mask * aff
    total = jnp.sum(top_aff, axis=0, keepdims=True)
    return mask, top_aff / total


def _outproj_kernel(x_ref, ret_ref, att_ref, w_ref, g_ref, b_ref, rw_ref, rb_ref,
                    x1_ref, x1p_ref, mask_ref, comb_ref):
    acc = jnp.dot(ret_ref[...], w_ref[:RET_WIDTH, :], preferred_element_type=F32)
    acc = acc + jnp.dot(att_ref[...], w_ref[RET_WIDTH:, :], preferred_element_type=F32)
    x1 = _layer_norm(DEEPNORM_ALPHA * x_ref[...] + acc, g_ref[...], b_ref[...])
    x1_ref[...] = x1
    x1p_ref[...] = _pack_rows(x1)
    x_hi = x1.astype(BF16)
    x_lo = (x1 - x_hi.astype(F32)).astype(BF16)
    rw = rw_ref[...]
    both = jnp.dot(x_hi, rw, preferred_element_type=F32)
    low = jnp.dot(x_lo, rw, preferred_element_type=F32)
    logits = both[:, :LANES] + both[:, LANES:] + low[:, :LANES]
    logits_t = logits.T[:N_EXPERTS, :]
    mask, comb = _route(logits_t, rb_ref[...])
    mask_ref[...] = mask
    comb_ref[...] = comb


def _outproj_ln_router(x2d, ret, att, w_bf, layer, ln_g, ln_b, rw_split, rb):
    t = x2d.shape[0]
    tm = min(OUTPROJ_TM, t)
    row = lambda w: pl.BlockSpec((tm, w), lambda i: (i, 0))
    const = lambda shape: pl.BlockSpec(shape, lambda i: (0,) * len(shape))
    w_spec = pl.BlockSpec((None, D_MODEL, D_MODEL), lambda i: (layer, 0, 0))
    lanes = pl.BlockSpec((N_EXPERTS, tm), lambda i: (0, i))
    return pl.pallas_call(
        _outproj_kernel,
        out_shape=(jax.ShapeDtypeStruct((t, D_MODEL), F32),
                   jax.ShapeDtypeStruct((t, HALF_ROW), U32),
                   jax.ShapeDtypeStruct((N_EXPERTS, t), F32),
                   jax.ShapeDtypeStruct((N_EXPERTS, t), F32)),
        grid=(t // tm,),
        in_specs=[row(D_MODEL), row(RET_WIDTH), row(ATT_WIDTH), w_spec,
                  const((1, D_MODEL)), const((1, D_MODEL)), const((D_MODEL, MXU_DIM)), const((N_EXPERTS, 1))],
        out_specs=(row(D_MODEL), row(HALF_ROW), lanes, lanes),
        compiler_params=_params(("parallel",)),
        name="outproj_ln_router",
    )(x2d, ret, att, w_bf, ln_g.reshape(1, D_MODEL), ln_b.reshape(1, D_MODEL), rw_split, rb.reshape(N_EXPERTS, 1))


def _split_router(router_w):
    hi = router_w.astype(BF16)
    lo = (router_w - hi.astype(F32)).astype(BF16)
    out = jnp.zeros((D_MODEL, MXU_DIM), BF16)
    return out.at[:, :N_EXPERTS].set(hi).at[:, LANES:LANES + N_EXPERTS].set(lo)


def _rank_kernel(mask_ref, rank_ref, count_ref, carry_ref):
    @pl.when(pl.program_id(0) == 0)
    def _():
        carry_ref[...] = jnp.zeros_like(carry_ref)

    m = mask_ref[...]
    tt = m.shape[1]
    upper = (lax.broadcasted_iota(I32, (tt, tt), 0) <= lax.broadcasted_iota(I32, (tt, tt), 1)).astype(BF16)
    incl = jnp.dot(m.astype(BF16), upper, preferred_element_type=F32)
    carry = carry_ref[...]
    rank_ref[...] = (carry[:, :1] + incl - m).astype(I32)
    carry = carry + jnp.sum(m, axis=1, keepdims=True)
    carry_ref[...] = carry
    count_ref[...] = carry


def _expert_ranks(mask_t):
    t = mask_t.shape[1]
    tt = min(RANK_TT, t)
    return pl.pallas_call(
        _rank_kernel,
        out_shape=(jax.ShapeDtypeStruct((N_EXPERTS, t), I32), jax.ShapeDtypeStruct((N_EXPERTS, LANES), F32)),
        grid=(t // tt,),
        in_specs=[pl.BlockSpec((N_EXPERTS, tt), lambda i: (0, i))],
        out_specs=(pl.BlockSpec((N_EXPERTS, tt), lambda i: (0, i)),
                   pl.BlockSpec((N_EXPERTS, LANES), lambda i: (0, 0))),
        scratch_shapes=[pltpu.VMEM((N_EXPERTS, LANES), F32)],
        compiler_params=_params(("arbitrary",)),
        name="expert_ranks",
    )(mask_t)


def _dispatch_plan(mask_t, rank_t, counts, comb_t, tm):
    t = mask_t.shape[1]
    n_tiles = 2 * t // tm + N_EXPERTS
    cnt = counts[:, 0].astype(I32)
    tiles = (cnt + tm - 1) // tm
    tile_end = jnp.cumsum(tiles)
    off = (tile_end - tiles) * tm
    sel = mask_t > 0.5
    pos_t = off[:, None] + rank_t
    pos_a = jnp.min(jnp.where(sel, pos_t, n_tiles * tm), axis=0)
    pos_b = jnp.max(jnp.where(sel, pos_t, -1), axis=0)
    e_id = jnp.arange(N_EXPERTS, dtype=I32)[:, None]
    e_a = jnp.min(jnp.where(sel, e_id, N_EXPERTS), axis=0)
    w_a = jnp.sum(jnp.where(sel & (e_id == e_a[None, :]), comb_t, 0.0), axis=0)
    w_b = jnp.sum(jnp.where(sel & (e_id != e_a[None, :]), comb_t, 0.0), axis=0)
    n_valid = tile_end[-1]
    tile_blk = jnp.minimum(jnp.arange(n_tiles, dtype=I32), n_valid - 1)
    tile_e = jnp.sum((tile_end[None, :] <= tile_blk[:, None]).astype(I32), axis=1)
    pad_end = (off + tiles * tm).at[N_EXPERTS - 1].set(n_tiles * tm)
    pad_rows = jnp.stack([off + cnt, pad_end]).astype(I32)
    return dict(pos_a=pos_a.astype(I32), pos_b=pos_b.astype(I32), w_a=w_a, w_b=w_b, n_valid=n_valid.astype(I32),
                tile_blk=tile_blk, tile_e=tile_e, pad_rows=pad_rows, n_tiles=n_tiles)


def _dispatch_kernel(pad_ref, pos_ref, x_ref, wg_ref, wu_ref, wd_ref, xs_hbm, wg_bf_ref, wu_bf_ref, wd_bf_ref,
                     zero_ref, sem):
    i = pl.program_id(0)
    tt = x_ref.shape[0]
    wg_bf_ref[...] = wg_ref[...].astype(BF16)
    wu_bf_ref[...] = wu_ref[...].astype(BF16)
    wd_bf_ref[...] = wd_ref[...].astype(BF16)

    def row_copy(src, dst):
        return pltpu.make_async_copy(src, dst, sem)

    @pl.when(i == 0)
    def _():
        zero_ref[...] = jnp.zeros_like(zero_ref)
        for e in range(N_EXPERTS):
            lo, hi = pad_ref[0, e], pad_ref[1, e]

            def start(r, c):
                row_copy(zero_ref.at[pl.ds(0, 1)], xs_hbm.at[pl.ds(r, 1)]).start()
                return c

            def wait(r, c):
                row_copy(zero_ref.at[pl.ds(0, 1)], xs_hbm.at[pl.ds(r, 1)]).wait()
                return c

            lax.fori_loop(lo, hi, start, 0)
            lax.fori_loop(lo, hi, wait, 0)

    def copies(j, u):
        src = x_ref.at[pl.ds(pl.multiple_of(j * SUBLANES, SUBLANES), SUBLANES)].at[pl.ds(u, 1)]
        k = j * SUBLANES + u
        return (row_copy(src, xs_hbm.at[pl.ds(pos_ref[0, 0, 2 * k], 1)]),
                row_copy(src, xs_hbm.at[pl.ds(pos_ref[0, 0, 2 * k + 1], 1)]))

    def start(j, c):
        for u in range(SUBLANES):
            a, b = copies(j, u)
            a.start(priority=0)
            b.start(priority=1)
        return c

    def wait(j, c):
        for u in range(SUBLANES):
            a, b = copies(j, u)
            a.wait()
            b.wait()
        return c

    lax.fori_loop(0, tt // SUBLANES, start, 0)
    lax.fori_loop(0, tt // SUBLANES, wait, 0)


def _dispatch(x1p, plan, pos, w_gate, w_up, w_down, layer):
    t = x1p.shape[0]
    n_steps, tt = pos.shape[0], pos.shape[2] // 2
    n_rows = plan["n_tiles"] * FFN_TM
    gu_rows = N_EXPERTS * D_MODEL // n_steps
    dn_rows = N_EXPERTS * D_FF // n_steps
    flat = lambda w: w.reshape(DEPTH, -1, w.shape[-1])
    w_in_spec = lambda rows, cols: pl.BlockSpec((None, rows, cols), lambda i, pad: (layer, i, 0))
    w_out_spec = lambda rows, cols: pl.BlockSpec((rows, cols), lambda i, pad: (i, 0))
    xs, wg_bf, wu_bf, wd_bf = pl.pallas_call(
        _dispatch_kernel,
        out_shape=(jax.ShapeDtypeStruct((n_rows, HALF_ROW), U32),
                   jax.ShapeDtypeStruct((N_EXPERTS * D_MODEL, D_FF), BF16),
                   jax.ShapeDtypeStruct((N_EXPERTS * D_MODEL, D_FF), BF16),
                   jax.ShapeDtypeStruct((N_EXPERTS * D_FF, D_MODEL), BF16)),
        grid_spec=pltpu.PrefetchScalarGridSpec(
            num_scalar_prefetch=1,
            grid=(n_steps,),
            in_specs=[pl.BlockSpec((1, 1, 2 * tt), lambda i, pad: (i, 0, 0), memory_space=pltpu.SMEM),
                      pl.BlockSpec((tt, HALF_ROW), lambda i, pad: (i, 0)),
                      w_in_spec(gu_rows, D_FF), w_in_spec(gu_rows, D_FF), w_in_spec(dn_rows, D_MODEL)],
            out_specs=(pl.BlockSpec(memory_space=pl.ANY),
                       w_out_spec(gu_rows, D_FF), w_out_spec(gu_rows, D_FF), w_out_spec(dn_rows, D_MODEL)),
            scratch_shapes=[pltpu.VMEM((SUBLANES, HALF_ROW), U32), pltpu.SemaphoreType.DMA(())]),
        compiler_params=_params(("arbitrary",)),
        name="dispatch",
    )(plan["pad_rows"], pos, x1p, flat(w_gate), flat(w_up), flat(w_down))
    return (xs, wg_bf.reshape(N_EXPERTS, D_MODEL, D_FF), wu_bf.reshape(N_EXPERTS, D_MODEL, D_FF),
            wd_bf.reshape(N_EXPERTS, D_FF, D_MODEL))


def _ffn_kernel(blk_ref, exp_ref, nvalid_ref, xs_ref, wg_ref, wu_ref, wd_ref, ys_ref):
    @pl.when(pl.program_id(0) >= nvalid_ref[0])
    def _():
        ys_ref[...] = jnp.zeros_like(ys_ref)

    @pl.when(pl.program_id(0) < nvalid_ref[0])
    def _():
        lo, hi = _unpack_rows(xs_ref[...])
        lo, hi = lo.astype(BF16), hi.astype(BF16)
        gate = jnp.dot(lo, wg_ref[:HALF_ROW, :], preferred_element_type=F32)
        gate = gate + jnp.dot(hi, wg_ref[HALF_ROW:, :], preferred_element_type=F32)
        up = jnp.dot(lo, wu_ref[:HALF_ROW, :], preferred_element_type=F32)
        up = up + jnp.dot(hi, wu_ref[HALF_ROW:, :], preferred_element_type=F32)
        hidden = (gate * jax.nn.sigmoid(gate) * up).astype(BF16)
        ys_ref[...] = _pack_rows(jnp.dot(hidden, wd_ref[...], preferred_element_type=F32))


def _expert_ffn(xs, plan, wg, wu, wd):
    tm = FFN_TM
    n_tiles = plan["n_tiles"]
    rows = pl.BlockSpec((tm, HALF_ROW), lambda i, blk, e, n: (blk[i], 0))
    w_spec = lambda k, n: pl.BlockSpec((None, k, n), lambda i, blk, e, nv: (e[i], 0, 0))
    return pl.pallas_call(
        _ffn_kernel,
        out_shape=jax.ShapeDtypeStruct((n_tiles * tm, HALF_ROW), U32),
        grid_spec=pltpu.PrefetchScalarGridSpec(
            num_scalar_prefetch=3,
            grid=(n_tiles,),
            in_specs=[rows, w_spec(D_MODEL, D_FF), w_spec(D_MODEL, D_FF), w_spec(D_FF, D_MODEL)],
            out_specs=pl.BlockSpec((tm, HALF_ROW), lambda i, blk, e, n: (i, 0))),
        compiler_params=_params(("arbitrary",)),
        name="expert_ffn",
    )(plan["tile_blk"], plan["tile_e"], plan["n_valid"].reshape(1), xs, wg, wu, wd)


def _combine_kernel(pos_ref, next_pos_ref, x1_ref, w_ref, g_ref, b_ref, ys_hbm, x2_ref, buf_ref, sem):
    i = pl.program_id(0)
    tt = x1_ref.shape[0]
    slot = i % 2

    def copy(p_ref, j, u, which, s):
        dst = buf_ref.at[s, which, pl.ds(pl.multiple_of(j * SUBLANES, SUBLANES), SUBLANES)].at[pl.ds(u, 1)]
        k = j * SUBLANES + u
        return pltpu.make_async_copy(ys_hbm.at[pl.ds(p_ref[0, 0, 2 * k + which], 1)], dst, sem.at[s])

    def gather(p_ref, s):
        def start(j, c):
            for u in range(SUBLANES):
                copy(p_ref, j, u, 0, s).start(priority=0)
                copy(p_ref, j, u, 1, s).start(priority=1)
            return c
        lax.fori_loop(0, tt // SUBLANES, start, 0)

    @pl.when(i == 0)
    def _():
        gather(pos_ref, slot)

    @pl.when(i + 1 < pl.num_programs(0))
    def _():
        gather(next_pos_ref, 1 - slot)

    def wait(j, c):
        for u in range(SUBLANES):
            copy(pos_ref, j, u, 0, slot).wait()
            copy(pos_ref, j, u, 1, slot).wait()
        return c

    lax.fori_loop(0, tt // SUBLANES, wait, 0)
    a_lo, a_hi = _unpack_rows(buf_ref[slot, 0])
    b_lo, b_hi = _unpack_rows(buf_ref[slot, 1])
    w = w_ref[...]
    w_a, w_b = w[:, 0:1], w[:, 1:2]
    moe = jnp.concatenate([w_a * a_lo + w_b * b_lo, w_a * a_hi + w_b * b_hi], axis=1)
    x2_ref[...] = _layer_norm(DEEPNORM_ALPHA * x1_ref[...] + moe, g_ref[...], b_ref[...])


def _combine_ln(x1, ys, plan, pos, ln_g, ln_b):
    t = x1.shape[0]
    n_steps, tt = pos.shape[0], pos.shape[2] // 2
    w = jnp.concatenate([plan["w_a"][:, None], plan["w_b"][:, None], jnp.zeros((t, LANES - 2), F32)], axis=1)
    const = lambda shape: pl.BlockSpec(shape, lambda i: (0,) * len(shape))
    return pl.pallas_call(
        _combine_kernel,
        out_shape=jax.ShapeDtypeStruct((t, D_MODEL), F32),
        grid=(n_steps,),
        in_specs=[pl.BlockSpec((1, 1, 2 * tt), lambda i: (i, 0, 0), memory_space=pltpu.SMEM),
                  pl.BlockSpec((1, 1, 2 * tt), lambda i: (jnp.minimum(i + 1, n_steps - 1), 0, 0),
                               memory_space=pltpu.SMEM),
                  pl.BlockSpec((tt, D_MODEL), lambda i: (i, 0)),
                  pl.BlockSpec((tt, LANES), lambda i: (i, 0)),
                  const((1, D_MODEL)), const((1, D_MODEL)),
                  pl.BlockSpec(memory_space=pl.ANY)],
        out_specs=pl.BlockSpec((tt, D_MODEL), lambda i: (i, 0)),
        scratch_shapes=[pltpu.VMEM((2, 2, tt, HALF_ROW), U32), pltpu.SemaphoreType.DMA((2,))],
        compiler_params=_params(("arbitrary",)),
        name="combine_ln",
    )(pos, pos, x1, w, ln_g.reshape(1, D_MODEL), ln_b.reshape(1, D_MODEL), ys)


def kernel(x, w_in, ret_norm_gain, rel_bias, w_out, ln1_g, ln1_b, router_w, router_b,
           w_gate, w_up, w_down, ln2_g, ln2_b):
    batch, seq, _ = x.shape
    t = batch * seq
    tabs = _retention_tables(seq)
    rw_split = _split_router(router_w)
    h = x.reshape(t, D_MODEL)
    tt = min(ROW_MOVE_TT, t)
    assert (N_EXPERTS * D_FF) % (t // tt) == 0, "each dispatch step converts an equal slice of the expert weights"
    w_in_bf, w_out_bf = w_in.astype(BF16), w_out.astype(BF16)
    for l in range(DEPTH):
        proj = _inproj(h, w_in_bf, l)
        ret = _retention(proj, tabs, ret_norm_gain[l], batch, seq)
        att = _band_attention(proj, _band_bias(rel_bias[l]), batch, seq)
        x1, x1p, mask_t, comb_t = _outproj_ln_router(h, ret, att, w_out_bf, l, ln1_g[l], ln1_b[l],
                                                     rw_split, router_b)
        rank_t, counts = _expert_ranks(mask_t)
        plan = _dispatch_plan(mask_t, rank_t, counts, comb_t, FFN_TM)
        pos = jnp.stack([plan["pos_a"], plan["pos_b"]], axis=1).reshape(t // tt, 1, 2 * tt)
        xs, wg_bf, wu_bf, wd_bf = _dispatch(x1p, plan, pos, w_gate, w_up, w_down, l)
        ys = _expert_ffn(xs, plan, wg_bf, wu_bf, wd_bf)
        h = _combine_ln(x1, ys, plan, pos, ln2_g[l], ln2_b[l])
    return h.reshape(batch, seq, D_MODEL)
```

```python
import functools

import jax
import jax.numpy as jnp
import numpy as np
from jax import lax
from jax.experimental import pallas as pl
from jax.experimental.pallas import tpu as pltpu

F32 = jnp.float32
BF16 = jnp.bfloat16
I32 = jnp.int32
U32 = jnp.uint32

D_MODEL = 2048
DEPTH = 4
CHUNK = 64
RET_WIDTH = 1024
RET_HEADS = 4
RET_HEAD_DIM = 256
ATT_WIDTH = 1024
ATT_HEAD_DIM = 64
ATT_HEADS = 16
N_PREV_CHUNKS = 8
REL_CLIP = 256
IN_WIDTH = 4 * RET_WIDTH + 3 * ATT_WIDTH
N_EXPERTS = 16
N_GROUPS = 4
EXPERTS_PER_GROUP = 4
D_FF = 1024
ROPE_BASE = 10000.0
LN_EPS = 1e-5
DEEPNORM_ALPHA = (2 * DEPTH) ** 0.25

LANES = 128
SUBLANES = 8
MXU_DIM = 256
VMEM_LIMIT_BYTES = 56 * 1024 * 1024

SUPER = 4 * CHUNK
ATT_GROUP = MXU_DIM // ATT_HEAD_DIM
KEY_SPAN = 3 * SUPER
HALF_ROW = D_MODEL // 2
MASK_VALUE = -1e30
LOG2E = 1.4426950408889634

INPROJ_TM = 1024
INPROJ_TN = 1024
RET_BATCH_PER_STEP = 8
ATT_GROUPS_PER_STEP = 4
OUTPROJ_TM = 512
RANK_TT = 512
DISPATCH_TT = 512
COMBINE_TT = 256
FFN_TM = 512


def _params(semantics):
    return pltpu.CompilerParams(dimension_semantics=semantics, vmem_limit_bytes=VMEM_LIMIT_BYTES)


def _layer_norm(y, g, b):
    mu = jnp.mean(y, axis=-1, keepdims=True)
    yc = y - mu
    var = jnp.mean(yc * yc, axis=-1, keepdims=True)
    return yc * lax.rsqrt(var + LN_EPS) * g + b


def _pack_rows(y):
    yb = y.astype(BF16).astype(F32)
    lo = lax.bitcast_convert_type(yb[:, :HALF_ROW], U32)
    hi = lax.bitcast_convert_type(yb[:, HALF_ROW:], U32)
    return (lo >> 16) | (hi & jnp.uint32(0xFFFF0000))


def _unpack_rows(w):
    lo = lax.bitcast_convert_type(w << 16, F32)
    hi = lax.bitcast_convert_type(w & jnp.uint32(0xFFFF0000), F32)
    return lo, hi


def _inproj_kernel(x_ref, w_ref, o_ref, xb_ref):
    @pl.when(pl.program_id(1) == 0)
    def _():
        xb_ref[...] = x_ref[...].astype(BF16)

    o_ref[...] = jnp.dot(xb_ref[...], w_ref[...], preferred_element_type=F32).astype(BF16)


def _inproj(x2d, w_bf, layer):
    t = x2d.shape[0]
    tm = min(INPROJ_TM, t)
    return pl.pallas_call(
        _inproj_kernel,
        out_shape=jax.ShapeDtypeStruct((t, IN_WIDTH), BF16),
        grid=(t // tm, IN_WIDTH // INPROJ_TN),
        in_specs=[pl.BlockSpec((tm, D_MODEL), lambda i, j: (i, 0)),
                  pl.BlockSpec((None, D_MODEL, INPROJ_TN), lambda i, j: (layer, 0, j))],
        out_specs=pl.BlockSpec((tm, INPROJ_TN), lambda i, j: (i, j)),
        scratch_shapes=[pltpu.VMEM((tm, D_MODEL), BF16)],
        compiler_params=_params(("parallel", "arbitrary")),
        name="inproj",
    )(x2d, w_bf)


def _rotate(t, cos, sin):
    t1, t2 = t[:, :LANES], t[:, LANES:]
    return t1 * cos - t2 * sin, t1 * sin + t2 * cos


def _retention_kernel(q_ref, k_ref, v_ref, g_ref, cos_ref, sin_ref, dmat_ref, qd_ref, kd_ref, cd_ref,
                      gain_ref, o_ref, state_ref):
    @pl.when(pl.program_id(2) == 0)
    def _():
        state_ref[...] = jnp.zeros_like(state_ref)

    cos, sin = cos_ref[...], sin_ref[...]
    qd, kd = qd_ref[...], kd_ref[...]
    scale = RET_HEAD_DIM ** -0.5
    for i in range(q_ref.shape[0]):
        q1, q2 = _rotate(q_ref[i].astype(F32), cos, sin)
        k1, k2 = _rotate(k_ref[i].astype(F32), cos, sin)
        k1, k2 = k1 * scale, k2 * scale
        v = v_ref[i]

        qb = jnp.concatenate([q1, q2], axis=1).astype(BF16)
        kb = jnp.concatenate([k1, k2], axis=1).astype(BF16)
        qs = jnp.concatenate([q1 * qd, q2 * qd], axis=1).astype(BF16)
        ks = jnp.concatenate([k1 * kd, k2 * kd], axis=1).astype(BF16)

        scores = lax.dot_general(qb, kb, (((1,), (1,)), ((), ())), preferred_element_type=F32) * dmat_ref[...]
        intra = jnp.dot(scores.astype(BF16), v, preferred_element_type=F32)
        state = state_ref[i]
        cross = jnp.dot(qs, state.astype(BF16), preferred_element_type=F32)
        kv = lax.dot_general(ks, v, (((0,), (0,)), ((), ())), preferred_element_type=F32)
        state_ref[i] = state * cd_ref[...] + kv

        o = intra + cross
        mu = jnp.mean(o, axis=-1, keepdims=True)
        oc = o - mu
        var = jnp.mean(oc * oc, axis=-1, keepdims=True)
        on = oc * lax.rsqrt(var + LN_EPS) * gain_ref[...]
        g = g_ref[i].astype(F32)
        o_ref[i] = (g * jax.nn.sigmoid(g) * on).astype(BF16)


def _retention_tables(seq):
    h = jnp.arange(RET_HEADS, dtype=F32)
    log_gamma = jnp.log1p(-jnp.exp2(-5.0 - h))
    r = jnp.arange(SUPER, dtype=F32)
    chunk = np.arange(SUPER) // CHUNK
    allowed = jnp.asarray(chunk[None, :] <= chunk[:, None])
    dist = jnp.abs(r[:, None] - r[None, :])
    dmat = jnp.where(allowed[None], jnp.exp(log_gamma[:, None, None] * dist[None]), 0.0)
    lane_bcast = lambda t: jnp.broadcast_to(t[:, :, None], (RET_HEADS, SUPER, LANES))
    qd = lane_bcast(jnp.exp(log_gamma[:, None] * (r + 1.0)[None, :]))
    kd = lane_bcast(jnp.exp(log_gamma[:, None] * (SUPER - 1.0 - r)[None, :]))
    cd = jnp.broadcast_to(jnp.exp(log_gamma * SUPER)[:, None, None], (RET_HEADS, 1, RET_HEAD_DIM))
    half = RET_HEAD_DIM // 2
    inv = ROPE_BASE ** (-jnp.arange(half, dtype=F32) / half)
    ang = jnp.arange(seq, dtype=F32)[:, None] * inv[None, :]
    return dict(cos=jnp.cos(ang), sin=jnp.sin(ang), dmat=dmat, qd=qd, kd=kd, cd=cd)


def _retention(proj, tabs, gain, batch, seq):
    n_sc = seq // SUPER
    nb = RET_BATCH_PER_STEP if batch % RET_BATCH_PER_STEP == 0 else 1
    proj = proj.reshape(batch, seq, IN_WIDTH)

    def col(off):
        return pl.BlockSpec((nb, SUPER, RET_HEAD_DIM), lambda b, h, s: (b, s, off + h))

    head_tab = lambda shape: pl.BlockSpec((None,) + shape, lambda b, h, s: (h, 0, 0))
    out = pl.pallas_call(
        _retention_kernel,
        out_shape=jax.ShapeDtypeStruct((batch, seq, RET_WIDTH), BF16),
        grid=(batch // nb, RET_HEADS, n_sc),
        in_specs=[col(0), col(RET_HEADS), col(2 * RET_HEADS), col(3 * RET_HEADS),
                  pl.BlockSpec((SUPER, LANES), lambda b, h, s: (s, 0)),
                  pl.BlockSpec((SUPER, LANES), lambda b, h, s: (s, 0)),
                  head_tab((SUPER, SUPER)), head_tab((SUPER, LANES)), head_tab((SUPER, LANES)),
                  head_tab((1, RET_HEAD_DIM)),
                  pl.BlockSpec((1, RET_HEAD_DIM), lambda b, h, s: (0, h))],
        out_specs=pl.BlockSpec((nb, SUPER, RET_HEAD_DIM), lambda b, h, s: (b, s, h)),
        scratch_shapes=[pltpu.VMEM((nb, RET_HEAD_DIM, RET_HEAD_DIM), F32)],
        compiler_params=_params(("parallel", "parallel", "arbitrary")),
        name="retention",
    )(proj, proj, proj, proj, tabs["cos"], tabs["sin"], tabs["dmat"], tabs["qd"], tabs["kd"], tabs["cd"],
      gain.reshape(1, RET_WIDTH))
    return out.reshape(batch * seq, RET_WIDTH)


def _band_attention_kernel(q_ref, k0_ref, k1_ref, k2_ref, v0_ref, v1_ref, v2_ref, bias_ref, o_ref):
    sc = pl.program_id(2)

    def attend(mask_early_blocks):
        head_of_lane = lax.broadcasted_iota(I32, (1, MXU_DIM), 1) // ATT_HEAD_DIM
        if mask_early_blocks:
            key_block = lax.broadcasted_iota(I32, (1, KEY_SPAN), 1) // SUPER
            in_sequence = (key_block + sc - 2) >= 0
        for gi in range(q_ref.shape[1] // MXU_DIM):
            cols = slice(gi * MXU_DIM, (gi + 1) * MXU_DIM)
            q = q_ref[:, cols]
            kt = jnp.concatenate([k0_ref[:, cols].T, k1_ref[:, cols].T, k2_ref[:, cols].T], axis=1)
            vcat = jnp.concatenate([v0_ref[:, cols], v1_ref[:, cols], v2_ref[:, cols]], axis=0)
            acc = jnp.zeros((SUPER, MXU_DIM), F32)
            for j in range(ATT_GROUP):
                mine = head_of_lane == j
                qj = jnp.where(mine, q, jnp.zeros_like(q))
                s = jnp.dot(qj, kt, preferred_element_type=F32)
                s = s * (ATT_HEAD_DIM ** -0.5 * LOG2E) + bias_ref[gi * ATT_GROUP + j]
                if mask_early_blocks:
                    s = jnp.where(in_sequence, s, MASK_VALUE)
                m = jnp.max(s, axis=-1, keepdims=True)
                p = jnp.exp2(s - m)
                denom = jnp.sum(p, axis=-1, keepdims=True)
                pv = jnp.dot(p.astype(BF16), vcat, preferred_element_type=F32)
                acc = jnp.where(mine, pv / denom, acc)
            o_ref[:, cols] = acc.astype(BF16)

    @pl.when(sc < 2)
    def _():
        attend(True)

    @pl.when(sc >= 2)
    def _():
        attend(False)


def _band_bias(rel_table):
    r = np.arange(SUPER)[:, None]
    c = np.arange(KEY_SPAN)[None, :]
    qi, kc = r // CHUNK, c // CHUNK
    in_band = (kc >= qi) & (kc <= qi + N_PREV_CHUNKS)
    period = SUPER + KEY_SPAN - 1
    n = np.arange(period)
    c_minus_r = np.where(n < KEY_SPAN, n, n - period)
    rel_idx = np.clip(2 * SUPER - c_minus_r, -(CHUNK - 1), REL_CLIP) + (CHUNK - 1)
    per_dist = rel_table[:, rel_idx].astype(F32)
    flat = jnp.tile(per_dist, (1, SUPER))[:, :SUPER * (period - 1)]
    toeplitz = flat.reshape(ATT_HEADS, SUPER, period - 1)[:, :, :KEY_SPAN]
    bias = jnp.where(jnp.asarray(in_band)[None], toeplitz * LOG2E, MASK_VALUE)
    heads_per_step = ATT_GROUP * ATT_GROUPS_PER_STEP
    return bias.reshape(ATT_HEADS // heads_per_step, heads_per_step, SUPER, KEY_SPAN)


def _band_attention(proj, bias, batch, seq):
    n_sc = seq // SUPER
    width = MXU_DIM * ATT_GROUPS_PER_STEP
    n_blk = ATT_WIDTH // width
    col0 = 4 * RET_WIDTH // width

    def spec(which, back):
        def imap(g, b, s):
            return (b * n_sc + jnp.maximum(s - back, 0), col0 + which * n_blk + g)
        return pl.BlockSpec((SUPER, width), imap)

    return pl.pallas_call(
        _band_attention_kernel,
        out_shape=jax.ShapeDtypeStruct((batch * seq, ATT_WIDTH), BF16),
        grid=(n_blk, batch, n_sc),
        in_specs=[spec(0, 0), spec(1, 2), spec(1, 1), spec(1, 0), spec(2, 2), spec(2, 1), spec(2, 0),
                  pl.BlockSpec((None, ATT_GROUP * ATT_GROUPS_PER_STEP, SUPER, KEY_SPAN), lambda g, b, s: (g, 0, 0, 0))],
        out_specs=pl.BlockSpec((SUPER, width), lambda g, b, s: (b * n_sc + s, g)),
        compiler_params=_params(("parallel", "parallel", "arbitrary")),
        name="band_attention",
    )(proj, proj, proj, proj, proj, proj, proj, bias)


def _route(logits_t, rb):
    aff = jax.nn.sigmoid(logits_t)
    sel = aff + rb
    rows = [sel[e:e + 1, :] for e in range(N_EXPERTS)]
    best_score, choice = None, None
    for g in range(N_GROUPS):
        r = rows[g * EXPERTS_PER_GROUP:(g + 1) * EXPERTS_PER_GROUP]
        pair = None
        for a in range(EXPERTS_PER_GROUP):
            for b in range(a + 1, EXPERTS_PER_GROUP):
                s = r[a] + r[b]
                pair = s if pair is None else jnp.maximum(pair, s)
        if g == 0:
            best_score, choice = pair, jnp.zeros_like(pair, dtype=I32)
        else:
            better = pair > best_score
            best_score = jnp.where(better, pair, best_score)
            choice = jnp.where(better, g, choice)
    masks = []
    for e in range(N_EXPERTS):
        g, base = e // EXPERTS_PER_GROUP, (e // EXPERTS_PER_GROUP) * EXPERTS_PER_GROUP
        beaten = jnp.zeros_like(choice)
        for o in range(base, base + EXPERTS_PER_GROUP):
            if o == e:
                continue
            ahead = (rows[o] > rows[e]) | ((rows[o] == rows[e]) & (o < e))
            beaten = beaten + ahead.astype(I32)
        masks.append(((choice == g) & (beaten < 2)).astype(F32))
    mask = jnp.concatenate(masks, axis=0)
    top_aff = mask * aff
    total = jnp.sum(top_aff, axis=0, keepdims=True)
    return mask, top_aff / total


def _outproj_kernel(x_ref, ret_ref, att_ref, w_ref, g_ref, b_ref, rw_ref, rb_ref,
                    x1_ref, x1p_ref, mask_ref, comb_ref):
    acc = jnp.dot(ret_ref[...], w_ref[:RET_WIDTH, :], preferred_element_type=F32)
    acc = acc + jnp.dot(att_ref[...], w_ref[RET_WIDTH:, :], preferred_element_type=F32)
    x1 = _layer_norm(DEEPNORM_ALPHA * x_ref[...] + acc, g_ref[...], b_ref[...])
    x1_ref[...] = x1
    x1p_ref[...] = _pack_rows(x1)
    x_hi = x1.astype(BF16)
    x_lo = (x1 - x_hi.astype(F32)).astype(BF16)
    rw = rw_ref[...]
    both = jnp.dot(x_hi, rw, preferred_element_type=F32)
    low = jnp.dot(x_lo, rw, preferred_element_type=F32)
    logits = both[:, :LANES] + both[:, LANES:] + low[:, :LANES]
    logits_t = logits.T[:N_EXPERTS, :]
    mask, comb = _route(logits_t, rb_ref[...])
    mask_ref[...] = mask
    comb_ref[...] = comb


def _outproj_ln_router(x2d, ret, att, w_bf, layer, ln_g, ln_b, rw_split, rb):
    t = x2d.shape[0]
    tm = min(OUTPROJ_TM, t)
    row = lambda w: pl.BlockSpec((tm, w), lambda i: (i, 0))
    const = lambda shape: pl.BlockSpec(shape, lambda i: (0,) * len(shape))
    w_spec = pl.BlockSpec((None, D_MODEL, D_MODEL), lambda i: (layer, 0, 0), pipeline_mode=pl.Buffered(1))
    lanes = pl.BlockSpec((N_EXPERTS, tm), lambda i: (0, i))
    return pl.pallas_call(
        _outproj_kernel,
        out_shape=(jax.ShapeDtypeStruct((t, D_MODEL), F32),
                   jax.ShapeDtypeStruct((t, HALF_ROW), U32),
                   jax.ShapeDtypeStruct((N_EXPERTS, t), F32),
                   jax.ShapeDtypeStruct((N_EXPERTS, t), F32)),
        grid=(t // tm,),
        in_specs=[row(D_MODEL), row(RET_WIDTH), row(ATT_WIDTH), w_spec,
                  const((1, D_MODEL)), const((1, D_MODEL)), const((D_MODEL, MXU_DIM)), const((N_EXPERTS, 1))],
        out_specs=(row(D_MODEL), row(HALF_ROW), lanes, lanes),
        compiler_params=_params(("parallel",)),
        name="outproj_ln_router",
    )(x2d, ret, att, w_bf, ln_g.reshape(1, D_MODEL), ln_b.reshape(1, D_MODEL), rw_split, rb.reshape(N_EXPERTS, 1))


def _split_router(router_w):
    hi = router_w.astype(BF16)
    lo = (router_w - hi.astype(F32)).astype(BF16)
    out = jnp.zeros((D_MODEL, MXU_DIM), BF16)
    return out.at[:, :N_EXPERTS].set(hi).at[:, LANES:LANES + N_EXPERTS].set(lo)


def _rank_kernel(mask_ref, rank_ref, count_ref, carry_ref):
    @pl.when(pl.program_id(0) == 0)
    def _():
        carry_ref[...] = jnp.zeros_like(carry_ref)

    m = mask_ref[...]
    tt = m.shape[1]
    upper = (lax.broadcasted_iota(I32, (tt, tt), 0) <= lax.broadcasted_iota(I32, (tt, tt), 1)).astype(BF16)
    incl = jnp.dot(m.astype(BF16), upper, preferred_element_type=F32)
    carry = carry_ref[...]
    rank_ref[...] = (carry[:, :1] + incl - m).astype(I32)
    carry = carry + jnp.sum(m, axis=1, keepdims=True)
    carry_ref[...] = carry
    count_ref[...] = carry


def _expert_ranks(mask_t):
    t = mask_t.shape[1]
    tt = min(RANK_TT, t)
    return pl.pallas_call(
        _rank_kernel,
        out_shape=(jax.ShapeDtypeStruct((N_EXPERTS, t), I32), jax.ShapeDtypeStruct((N_EXPERTS, LANES), F32)),
        grid=(t // tt,),
        in_specs=[pl.BlockSpec((N_EXPERTS, tt), lambda i: (0, i))],
        out_specs=(pl.BlockSpec((N_EXPERTS, tt), lambda i: (0, i)),
                   pl.BlockSpec((N_EXPERTS, LANES), lambda i: (0, 0))),
        scratch_shapes=[pltpu.VMEM((N_EXPERTS, LANES), F32)],
        compiler_params=_params(("arbitrary",)),
        name="expert_ranks",
    )(mask_t)


def _dispatch_plan(mask_t, rank_t, counts, comb_t, tm):
    t = mask_t.shape[1]
    n_tiles = 2 * t // tm + N_EXPERTS
    cnt = counts[:, 0].astype(I32)
    tiles = (cnt + tm - 1) // tm
    tile_end = jnp.cumsum(tiles)
    off = (tile_end - tiles) * tm
    sel = mask_t > 0.5
    pos_t = off[:, None] + rank_t
    pos_a = jnp.min(jnp.where(sel, pos_t, n_tiles * tm), axis=0)
    pos_b = jnp.max(jnp.where(sel, pos_t, -1), axis=0)
    e_id = jnp.arange(N_EXPERTS, dtype=I32)[:, None]
    e_a = jnp.min(jnp.where(sel, e_id, N_EXPERTS), axis=0)
    w_a = jnp.sum(jnp.where(sel & (e_id == e_a[None, :]), comb_t, 0.0), axis=0)
    w_b = jnp.sum(jnp.where(sel & (e_id != e_a[None, :]), comb_t, 0.0), axis=0)
    n_valid = tile_end[-1]
    tile_blk = jnp.minimum(jnp.arange(n_tiles, dtype=I32), n_valid - 1)
    tile_e = jnp.sum((tile_end[None, :] <= tile_blk[:, None]).astype(I32), axis=1)
    pad_end = (off + tiles * tm).at[N_EXPERTS - 1].set(n_tiles * tm)
    pad_rows = jnp.stack([off + cnt, pad_end]).astype(I32)
    return dict(pos_a=pos_a.astype(I32), pos_b=pos_b.astype(I32), w_a=w_a, w_b=w_b, n_valid=n_valid.astype(I32),
                tile_blk=tile_blk, tile_e=tile_e, pad_rows=pad_rows, n_tiles=n_tiles)


def _dispatch_kernel(pad_ref, pos_ref, x_ref, wg_ref, wu_ref, wd_ref, xs_hbm, wg_bf_ref, wu_bf_ref, wd_bf_ref,
                     zero_ref, sem):
    i = pl.program_id(0)
    tt = x_ref.shape[0]
    wg_bf_ref[...] = wg_ref[...].astype(BF16)
    wu_bf_ref[...] = wu_ref[...].astype(BF16)
    wd_bf_ref[...] = wd_ref[...].astype(BF16)

    def row_copy(src, dst):
        return pltpu.make_async_copy(src, dst, sem)

    @pl.when(i == 0)
    def _():
        zero_ref[...] = jnp.zeros_like(zero_ref)
        for e in range(N_EXPERTS):
            lo, hi = pad_ref[0, e], pad_ref[1, e]

            def start(r, c):
                row_copy(zero_ref.at[pl.ds(0, 1)], xs_hbm.at[pl.ds(r, 1)]).start()
                return c

            def wait(r, c):
                row_copy(zero_ref.at[pl.ds(0, 1)], xs_hbm.at[pl.ds(r, 1)]).wait()
                return c

            lax.fori_loop(lo, hi, start, 0)
            lax.fori_loop(lo, hi, wait, 0)

    def copies(j, u):
        src = x_ref.at[pl.ds(pl.multiple_of(j * SUBLANES, SUBLANES), SUBLANES)].at[pl.ds(u, 1)]
        k = j * SUBLANES + u
        return (row_copy(src, xs_hbm.at[pl.ds(pos_ref[0, 0, 2 * k], 1)]),
                row_copy(src, xs_hbm.at[pl.ds(pos_ref[0, 0, 2 * k + 1], 1)]))

    def start(j, c):
        for u in range(SUBLANES):
            a, b = copies(j, u)
            a.start(priority=0)
            b.start(priority=1)
        return c

    def wait(j, c):
        for u in range(SUBLANES):
            a, b = copies(j, u)
            a.wait()
            b.wait()
        return c

    lax.fori_loop(0, tt // SUBLANES, start, 0)
    lax.fori_loop(0, tt // SUBLANES, wait, 0)


def _dispatch(x1p, plan, pos, w_gate, w_up, w_down, layer):
    t = x1p.shape[0]
    n_steps, tt = pos.shape[0], pos.shape[2] // 2
    n_rows = plan["n_tiles"] * FFN_TM
    gu_rows = N_EXPERTS * D_MODEL // n_steps
    dn_rows = N_EXPERTS * D_FF // n_steps
    flat = lambda w: w.reshape(DEPTH, -1, w.shape[-1])
    w_in_spec = lambda rows, cols: pl.BlockSpec((None, rows, cols), lambda i, pad: (layer, i, 0))
    w_out_spec = lambda rows, cols: pl.BlockSpec((rows, cols), lambda i, pad: (i, 0))
    xs, wg_bf, wu_bf, wd_bf = pl.pallas_call(
        _dispatch_kernel,
        out_shape=(jax.ShapeDtypeStruct((n_rows, HALF_ROW), U32),
                   jax.ShapeDtypeStruct((N_EXPERTS * D_MODEL, D_FF), BF16),
                   jax.ShapeDtypeStruct((N_EXPERTS * D_MODEL, D_FF), BF16),
                   jax.ShapeDtypeStruct((N_EXPERTS * D_FF, D_MODEL), BF16)),
        grid_spec=pltpu.PrefetchScalarGridSpec(
            num_scalar_prefetch=1,
            grid=(n_steps,),
            in_specs=[pl.BlockSpec((1, 1, 2 * tt), lambda i, pad: (i, 0, 0), memory_space=pltpu.SMEM),
                      pl.BlockSpec((tt, HALF_ROW), lambda i, pad: (i, 0)),
                      w_in_spec(gu_rows, D_FF), w_in_spec(gu_rows, D_FF), w_in_spec(dn_rows, D_MODEL)],
            out_specs=(pl.BlockSpec(memory_space=pl.ANY),
                       w_out_spec(gu_rows, D_FF), w_out_spec(gu_rows, D_FF), w_out_spec(dn_rows, D_MODEL)),
            scratch_shapes=[pltpu.VMEM((SUBLANES, HALF_ROW), U32), pltpu.SemaphoreType.DMA(())]),
        compiler_params=_params(("arbitrary",)),
        name="dispatch",
    )(plan["pad_rows"], pos, x1p, flat(w_gate), flat(w_up), flat(w_down))
    return (xs, wg_bf.reshape(N_EXPERTS, D_MODEL, D_FF), wu_bf.reshape(N_EXPERTS, D_MODEL, D_FF),
            wd_bf.reshape(N_EXPERTS, D_FF, D_MODEL))


def _ffn_kernel(blk_ref, exp_ref, nvalid_ref, xs_ref, wg_ref, wu_ref, wd_ref, ys_ref):
    @pl.when(pl.program_id(0) >= nvalid_ref[0])
    def _():
        ys_ref[...] = jnp.zeros_like(ys_ref)

    @pl.when(pl.program_id(0) < nvalid_ref[0])
    def _():
        lo, hi = _unpack_rows(xs_ref[...])
        lo, hi = lo.astype(BF16), hi.astype(BF16)
        gate = jnp.dot(lo, wg_ref[:HALF_ROW, :], preferred_element_type=F32)
        gate = gate + jnp.dot(hi, wg_ref[HALF_ROW:, :], preferred_element_type=F32)
        up = jnp.dot(lo, wu_ref[:HALF_ROW, :], preferred_element_type=F32)
        up = up + jnp.dot(hi, wu_ref[HALF_ROW:, :], preferred_element_type=F32)
        hidden = (gate * jax.nn.sigmoid(gate) * up).astype(BF16)
        ys_ref[...] = _pack_rows(jnp.dot(hidden, wd_ref[...], preferred_element_type=F32))


def _expert_ffn(xs, plan, wg, wu, wd):
    tm = FFN_TM
    n_tiles = plan["n_tiles"]
    rows = pl.BlockSpec((tm, HALF_ROW), lambda i, blk, e, n: (blk[i], 0))
    w_spec = lambda k, n: pl.BlockSpec((None, k, n), lambda i, blk, e, nv: (e[i], 0, 0))
    return pl.pallas_call(
        _ffn_kernel,
        out_shape=jax.ShapeDtypeStruct((n_tiles * tm, HALF_ROW), U32),
        grid_spec=pltpu.PrefetchScalarGridSpec(
            num_scalar_prefetch=3,
            grid=(n_tiles,),
            in_specs=[rows, w_spec(D_MODEL, D_FF), w_spec(D_MODEL, D_FF), w_spec(D_FF, D_MODEL)],
            out_specs=pl.BlockSpec((tm, HALF_ROW), lambda i, blk, e, n: (i, 0))),
        compiler_params=_params(("arbitrary",)),
        name="expert_ffn",
    )(plan["tile_blk"], plan["tile_e"], plan["n_valid"].reshape(1), xs, wg, wu, wd)


def _combine_kernel(pos_ref, next_pos_ref, x1_ref, w_ref, g_ref, b_ref, ys_hbm, x2_ref, buf_ref, sem):
    i = pl.program_id(0)
    tt = x1_ref.shape[0]
    slot = i % 2

    def copy(p_ref, j, u, which, s):
        dst = buf_ref.at[s, which, pl.ds(pl.multiple_of(j * SUBLANES, SUBLANES), SUBLANES)].at[pl.ds(u, 1)]
        k = j * SUBLANES + u
        return pltpu.make_async_copy(ys_hbm.at[pl.ds(p_ref[0, 0, 2 * k + which], 1)], dst, sem.at[s])

    def gather(p_ref, s):
        def start(j, c):
            for u in range(SUBLANES):
                copy(p_ref, j, u, 0, s).start(priority=0)
                copy(p_ref, j, u, 1, s).start(priority=1)
            return c
        lax.fori_loop(0, tt // SUBLANES, start, 0)

    @pl.when(i == 0)
    def _():
        gather(pos_ref, slot)

    @pl.when(i + 1 < pl.num_programs(0))
    def _():
        gather(next_pos_ref, 1 - slot)

    def wait(j, c):
        for u in range(SUBLANES):
            copy(pos_ref, j, u, 0, slot).wait()
            copy(pos_ref, j, u, 1, slot).wait()
        return c

    lax.fori_loop(0, tt // SUBLANES, wait, 0)
    a_lo, a_hi = _unpack_rows(buf_ref[slot, 0])
    b_lo, b_hi = _unpack_rows(buf_ref[slot, 1])
    w = w_ref[...]
    w_a, w_b = w[:, 0:1], w[:, 1:2]
    moe = jnp.concatenate([w_a * a_lo + w_b * b_lo, w_a * a_hi + w_b * b_hi], axis=1)
    x2_ref[...] = _layer_norm(DEEPNORM_ALPHA * x1_ref[...] + moe, g_ref[...], b_ref[...])


def _combine_ln(x1, ys, plan, pos, ln_g, ln_b):
    t = x1.shape[0]
    n_steps, tt = pos.shape[0], pos.shape[2] // 2
    w = jnp.concatenate([plan["w_a"][:, None], plan["w_b"][:, None], jnp.zeros((t, LANES - 2), F32)], axis=1)
    const = lambda shape: pl.BlockSpec(shape, lambda i: (0,) * len(shape))
    return pl.pallas_call(
        _combine_kernel,
        out_shape=jax.ShapeDtypeStruct((t, D_MODEL), F32),
        grid=(n_steps,),
        in_specs=[pl.BlockSpec((1, 1, 2 * tt), lambda i: (i, 0, 0), memory_space=pltpu.SMEM),
                  pl.BlockSpec((1, 1, 2 * tt), lambda i: (jnp.minimum(i + 1, n_steps - 1), 0, 0),
                               memory_space=pltpu.SMEM),
                  pl.BlockSpec((tt, D_MODEL), lambda i: (i, 0)),
                  pl.BlockSpec((tt, LANES), lambda i: (i, 0)),
                  const((1, D_MODEL)), const((1, D_MODEL)),
                  pl.BlockSpec(memory_space=pl.ANY)],
        out_specs=pl.BlockSpec((tt, D_MODEL), lambda i: (i, 0)),
        scratch_shapes=[pltpu.VMEM((2, 2, tt, HALF_ROW), U32), pltpu.SemaphoreType.DMA((2,))],
        compiler_params=_params(("arbitrary",)),
        name="combine_ln",
    )(pos, pos, x1, w, ln_g.reshape(1, D_MODEL), ln_b.reshape(1, D_MODEL), ys)


def kernel(x, w_in, ret_norm_gain, rel_bias, w_out, ln1_g, ln1_b, router_w, router_b,
           w_gate, w_up, w_down, ln2_g, ln2_b):
    batch, seq, _ = x.shape
    t = batch * seq
    tabs = _retention_tables(seq)
    rw_split = _split_router(router_w)
    h = x.reshape(t, D_MODEL)
    tt_d, tt_c = min(DISPATCH_TT, t), min(COMBINE_TT, t)
    assert (N_EXPERTS * D_FF) % (t // tt_d) == 0, "each dispatch step converts an equal slice of the expert weights"
    w_in_bf, w_out_bf = w_in.astype(BF16), w_out.astype(BF16)
    for l in range(DEPTH):
        proj = _inproj(h, w_in_bf, l)
        ret = _retention(proj, tabs, ret_norm_gain[l], batch, seq)
        att = _band_attention(proj, _band_bias(rel_bias[l]), batch, seq)
        x1, x1p, mask_t, comb_t = _outproj_ln_router(h, ret, att, w_out_bf, l, ln1_g[l], ln1_b[l],
                                                     rw_split, router_b)
        rank_t, counts = _expert_ranks(mask_t)
        plan = _dispatch_plan(mask_t, rank_t, counts, comb_t, FFN_TM)
        pos = jnp.stack([plan["pos_a"], plan["pos_b"]], axis=1)
        xs, wg_bf, wu_bf, wd_bf = _dispatch(x1p, plan, pos.reshape(t // tt_d, 1, 2 * tt_d), w_gate, w_up, w_down, l)
        ys = _expert_ffn(xs, plan, wg_bf, wu_bf, wd_bf)
        h = _combine_ln(x1, ys, plan, pos.reshape(t // tt_c, 1, 2 * tt_c), ln2_g[l], ln2_b[l])
    return h.reshape(batch, seq, D_MODEL)
```

```python
import functools

import jax
import jax.numpy as jnp
import numpy as np
from jax import lax
from jax.experimental import pallas as pl
from jax.experimental.pallas import tpu as pltpu

F32 = jnp.float32
BF16 = jnp.bfloat16
I32 = jnp.int32
U32 = jnp.uint32

D_MODEL = 2048
DEPTH = 4
CHUNK = 64
RET_WIDTH = 1024
RET_HEADS = 4
RET_HEAD_DIM = 256
ATT_WIDTH = 1024
ATT_HEAD_DIM = 64
ATT_HEADS = 16
N_PREV_CHUNKS = 8
REL_CLIP = 256
IN_WIDTH = 4 * RET_WIDTH + 3 * ATT_WIDTH
N_EXPERTS = 16
N_GROUPS = 4
EXPERTS_PER_GROUP = 4
D_FF = 1024
ROPE_BASE = 10000.0
LN_EPS = 1e-5
DEEPNORM_ALPHA = (2 * DEPTH) ** 0.25

LANES = 128
SUBLANES = 8
MXU_DIM = 256
VMEM_LIMIT_BYTES = 56 * 1024 * 1024

SUPER = 4 * CHUNK
ATT_GROUP = MXU_DIM // ATT_HEAD_DIM
KEY_SPAN = 3 * SUPER
HALF_ROW = D_MODEL // 2
MASK_VALUE = -1e30
LOG2E = 1.4426950408889634

INPROJ_TM = 1024
INPROJ_TN = 1792
RET_BATCH_PER_STEP = 8
ATT_GROUPS_PER_STEP = 4
OUTPROJ_TM = 512
RANK_TT = 512
DISPATCH_TT = 512
COMBINE_TT = 256
FFN_TM = 512


def _params(semantics):
    return pltpu.CompilerParams(dimension_semantics=semantics, vmem_limit_bytes=VMEM_LIMIT_BYTES)


def _layer_norm(y, g, b):
    mu = jnp.mean(y, axis=-1, keepdims=True)
    yc = y - mu
    var = jnp.mean(yc * yc, axis=-1, keepdims=True)
    return yc * lax.rsqrt(var + LN_EPS) * g + b


def _pack_rows(y):
    yb = y.astype(BF16).astype(F32)
    lo = lax.bitcast_convert_type(yb[:, :HALF_ROW], U32)
    hi = lax.bitcast_convert_type(yb[:, HALF_ROW:], U32)
    return (lo >> 16) | (hi & jnp.uint32(0xFFFF0000))


def _unpack_rows(w):
    lo = lax.bitcast_convert_type(w << 16, F32)
    hi = lax.bitcast_convert_type(w & jnp.uint32(0xFFFF0000), F32)
    return lo, hi


def _inproj_kernel(x_ref, w_ref, o_ref, xb_ref):
    @pl.when(pl.program_id(1) == 0)
    def _():
        xb_ref[...] = x_ref[...].astype(BF16)

    o_ref[...] = jnp.dot(xb_ref[...], w_ref[...], preferred_element_type=F32).astype(BF16)


def _inproj(x2d, w_bf, layer):
    t = x2d.shape[0]
    tm = min(INPROJ_TM, t)
    return pl.pallas_call(
        _inproj_kernel,
        out_shape=jax.ShapeDtypeStruct((t, IN_WIDTH), BF16),
        grid=(t // tm, IN_WIDTH // INPROJ_TN),
        in_specs=[pl.BlockSpec((tm, D_MODEL), lambda i, j: (i, 0)),
                  pl.BlockSpec((None, D_MODEL, INPROJ_TN), lambda i, j: (layer, 0, j))],
        out_specs=pl.BlockSpec((tm, INPROJ_TN), lambda i, j: (i, j)),
        scratch_shapes=[pltpu.VMEM((tm, D_MODEL), BF16)],
        compiler_params=_params(("parallel", "arbitrary")),
        name="inproj",
    )(x2d, w_bf)


def _rotate(t, cos, sin):
    t1, t2 = t[:, :LANES], t[:, LANES:]
    return t1 * cos - t2 * sin, t1 * sin + t2 * cos


def _retention_kernel(q_ref, k_ref, v_ref, g_ref, cos_ref, sin_ref, dmat_ref, qd_ref, kd_ref, cd_ref,
                      gain_ref, o_ref, state_ref):
    @pl.when(pl.program_id(2) == 0)
    def _():
        state_ref[...] = jnp.zeros_like(state_ref)

    cos, sin = cos_ref[...], sin_ref[...]
    qd, kd = qd_ref[...], kd_ref[...]
    scale = RET_HEAD_DIM ** -0.5
    for i in range(q_ref.shape[0]):
        q1, q2 = _rotate(q_ref[i].astype(F32), cos, sin)
        k1, k2 = _rotate(k_ref[i].astype(F32), cos, sin)
        k1, k2 = k1 * scale, k2 * scale
        v = v_ref[i]

        qb = jnp.concatenate([q1, q2], axis=1).astype(BF16)
        kb = jnp.concatenate([k1, k2], axis=1).astype(BF16)
        qs = jnp.concatenate([q1 * qd, q2 * qd], axis=1).astype(BF16)
        ks = jnp.concatenate([k1 * kd, k2 * kd], axis=1).astype(BF16)

        scores = lax.dot_general(qb, kb, (((1,), (1,)), ((), ())), preferred_element_type=F32) * dmat_ref[...]
        intra = jnp.dot(scores.astype(BF16), v, preferred_element_type=F32)
        state = state_ref[i]
        cross = jnp.dot(qs, state.astype(BF16), preferred_element_type=F32)
        kv = lax.dot_general(ks, v, (((0,), (0,)), ((), ())), preferred_element_type=F32)
        state_ref[i] = state * cd_ref[...] + kv

        o = intra + cross
        mu = jnp.mean(o, axis=-1, keepdims=True)
        oc = o - mu
        var = jnp.mean(oc * oc, axis=-1, keepdims=True)
        on = oc * lax.rsqrt(var + LN_EPS) * gain_ref[...]
        g = g_ref[i].astype(F32)
        o_ref[i] = (g * jax.nn.sigmoid(g) * on).astype(BF16)


def _retention_tables(seq):
    h = jnp.arange(RET_HEADS, dtype=F32)
    log_gamma = jnp.log1p(-jnp.exp2(-5.0 - h))
    r = jnp.arange(SUPER, dtype=F32)
    chunk = np.arange(SUPER) // CHUNK
    allowed = jnp.asarray(chunk[None, :] <= chunk[:, None])
    dist = jnp.abs(r[:, None] - r[None, :])
    dmat = jnp.where(allowed[None], jnp.exp(log_gamma[:, None, None] * dist[None]), 0.0)
    lane_bcast = lambda t: jnp.broadcast_to(t[:, :, None], (RET_HEADS, SUPER, LANES))
    qd = lane_bcast(jnp.exp(log_gamma[:, None] * (r + 1.0)[None, :]))
    kd = lane_bcast(jnp.exp(log_gamma[:, None] * (SUPER - 1.0 - r)[None, :]))
    cd = jnp.broadcast_to(jnp.exp(log_gamma * SUPER)[:, None, None], (RET_HEADS, 1, RET_HEAD_DIM))
    half = RET_HEAD_DIM // 2
    inv = ROPE_BASE ** (-jnp.arange(half, dtype=F32) / half)
    ang = jnp.arange(seq, dtype=F32)[:, None] * inv[None, :]
    return dict(cos=jnp.cos(ang), sin=jnp.sin(ang), dmat=dmat, qd=qd, kd=kd, cd=cd)


def _retention(proj, tabs, gain, batch, seq):
    n_sc = seq // SUPER
    nb = RET_BATCH_PER_STEP if batch % RET_BATCH_PER_STEP == 0 else 1
    proj = proj.reshape(batch, seq, IN_WIDTH)

    def col(off):
        return pl.BlockSpec((nb, SUPER, RET_HEAD_DIM), lambda b, h, s: (b, s, off + h))

    head_tab = lambda shape: pl.BlockSpec((None,) + shape, lambda b, h, s: (h, 0, 0))
    out = pl.pallas_call(
        _retention_kernel,
        out_shape=jax.ShapeDtypeStruct((batch, seq, RET_WIDTH), BF16),
        grid=(batch // nb, RET_HEADS, n_sc),
        in_specs=[col(0), col(RET_HEADS), col(2 * RET_HEADS), col(3 * RET_HEADS),
                  pl.BlockSpec((SUPER, LANES), lambda b, h, s: (s, 0)),
                  pl.BlockSpec((SUPER, LANES), lambda b, h, s: (s, 0)),
                  head_tab((SUPER, SUPER)), head_tab((SUPER, LANES)), head_tab((SUPER, LANES)),
                  head_tab((1, RET_HEAD_DIM)),
                  pl.BlockSpec((1, RET_HEAD_DIM), lambda b, h, s: (0, h))],
        out_specs=pl.BlockSpec((nb, SUPER, RET_HEAD_DIM), lambda b, h, s: (b, s, h)),
        scratch_shapes=[pltpu.VMEM((nb, RET_HEAD_DIM, RET_HEAD_DIM), F32)],
        compiler_params=_params(("parallel", "parallel", "arbitrary")),
        name="retention",
    )(proj, proj, proj, proj, tabs["cos"], tabs["sin"], tabs["dmat"], tabs["qd"], tabs["kd"], tabs["cd"],
      gain.reshape(1, RET_WIDTH))
    return out.reshape(batch * seq, RET_WIDTH)


def _band_attention_kernel(q_ref, k0_ref, k1_ref, k2_ref, v0_ref, v1_ref, v2_ref, bias_ref, o_ref):
    sc = pl.program_id(2)

    def attend(mask_early_blocks):
        head_of_lane = lax.broadcasted_iota(I32, (1, MXU_DIM), 1) // ATT_HEAD_DIM
        if mask_early_blocks:
            key_block = lax.broadcasted_iota(I32, (1, KEY_SPAN), 1) // SUPER
            in_sequence = (key_block + sc - 2) >= 0
        for gi in range(q_ref.shape[1] // MXU_DIM):
            cols = slice(gi * MXU_DIM, (gi + 1) * MXU_DIM)
            q = q_ref[:, cols]
            kt = jnp.concatenate([k0_ref[:, cols].T, k1_ref[:, cols].T, k2_ref[:, cols].T], axis=1)
            vcat = jnp.concatenate([v0_ref[:, cols], v1_ref[:, cols], v2_ref[:, cols]], axis=0)
            acc = jnp.zeros((SUPER, MXU_DIM), F32)
            for j in range(ATT_GROUP):
                mine = head_of_lane == j
                qj = jnp.where(mine, q, jnp.zeros_like(q))
                s = jnp.dot(qj, kt, preferred_element_type=F32)
                s = s * (ATT_HEAD_DIM ** -0.5 * LOG2E) + bias_ref[gi * ATT_GROUP + j]
                if mask_early_blocks:
                    s = jnp.where(in_sequence, s, MASK_VALUE)
                m = jnp.max(s, axis=-1, keepdims=True)
                p = jnp.exp2(s - m)
                denom = jnp.sum(p, axis=-1, keepdims=True)
                pv = jnp.dot(p.astype(BF16), vcat, preferred_element_type=F32)
                acc = jnp.where(mine, pv / denom, acc)
            o_ref[:, cols] = acc.astype(BF16)

    @pl.when(sc < 2)
    def _():
        attend(True)

    @pl.when(sc >= 2)
    def _():
        attend(False)


def _band_bias(rel_table):
    r = np.arange(SUPER)[:, None]
    c = np.arange(KEY_SPAN)[None, :]
    qi, kc = r // CHUNK, c // CHUNK
    in_band = (kc >= qi) & (kc <= qi + N_PREV_CHUNKS)
    period = SUPER + KEY_SPAN - 1
    n = np.arange(period)
    c_minus_r = np.where(n < KEY_SPAN, n, n - period)
    rel_idx = np.clip(2 * SUPER - c_minus_r, -(CHUNK - 1), REL_CLIP) + (CHUNK - 1)
    per_dist = rel_table[:, rel_idx].astype(F32)
    flat = jnp.tile(per_dist, (1, SUPER))[:, :SUPER * (period - 1)]
    toeplitz = flat.reshape(ATT_HEADS, SUPER, period - 1)[:, :, :KEY_SPAN]
    bias = jnp.where(jnp.asarray(in_band)[None], toeplitz * LOG2E, MASK_VALUE)
    heads_per_step = ATT_GROUP * ATT_GROUPS_PER_STEP
    return bias.reshape(ATT_HEADS // heads_per_step, heads_per_step, SUPER, KEY_SPAN)


def _band_attention(proj, bias, batch, seq):
    n_sc = seq // SUPER
    width = MXU_DIM * ATT_GROUPS_PER_STEP
    n_blk = ATT_WIDTH // width
    col0 = 4 * RET_WIDTH // width

    def spec(which, back):
        def imap(g, b, s):
            return (b * n_sc + jnp.maximum(s - back, 0), col0 + which * n_blk + g)
        return pl.BlockSpec((SUPER, width), imap)

    return pl.pallas_call(
        _band_attention_kernel,
        out_shape=jax.ShapeDtypeStruct((batch * seq, ATT_WIDTH), BF16),
        grid=(n_blk, batch, n_sc),
        in_specs=[spec(0, 0), spec(1, 2), spec(1, 1), spec(1, 0), spec(2, 2), spec(2, 1), spec(2, 0),
                  pl.BlockSpec((None, ATT_GROUP * ATT_GROUPS_PER_STEP, SUPER, KEY_SPAN), lambda g, b, s: (g, 0, 0, 0))],
        out_specs=pl.BlockSpec((SUPER, width), lambda g, b, s: (b * n_sc + s, g)),
        compiler_params=_params(("parallel", "parallel", "arbitrary")),
        name="band_attention",
    )(proj, proj, proj, proj, proj, proj, proj, bias)


def _route(logits_t, rb):
    aff = jax.nn.sigmoid(logits_t)
    sel = aff + rb
    rows = [sel[e:e + 1, :] for e in range(N_EXPERTS)]
    best_score, choice = None, None
    for g in range(N_GROUPS):
        r = rows[g * EXPERTS_PER_GROUP:(g + 1) * EXPERTS_PER_GROUP]
        pair = None
        for a in range(EXPERTS_PER_GROUP):
            for b in range(a + 1, EXPERTS_PER_GROUP):
                s = r[a] + r[b]
                pair = s if pair is None else jnp.maximum(pair, s)
        if g == 0:
            best_score, choice = pair, jnp.zeros_like(pair, dtype=I32)
        else:
            better = pair > best_score
            best_score = jnp.where(better, pair, best_score)
            choice = jnp.where(better, g, choice)
    masks = []
    for e in range(N_EXPERTS):
        g, base = e // EXPERTS_PER_GROUP, (e // EXPERTS_PER_GROUP) * EXPERTS_PER_GROUP
        beaten = jnp.zeros_like(choice)
        for o in range(base, base + EXPERTS_PER_GROUP):
            if o == e:
                continue
            ahead = (rows[o] > rows[e]) | ((rows[o] == rows[e]) & (o < e))
            beaten = beaten + ahead.astype(I32)
        masks.append(((choice == g) & (beaten < 2)).astype(F32))
    mask = jnp.concatenate(masks, axis=0)
    top_aff = mask * aff
    total = jnp.sum(top_aff, axis=0, keepdims=True)
    return mask, top_aff / total


def _outproj_kernel(x_ref, ret_ref, att_ref, w_ref, g_ref, b_ref, rw_ref, rb_ref,
                    x1_ref, x1p_ref, mask_ref, comb_ref):
    acc = jnp.dot(ret_ref[...], w_ref[:RET_WIDTH, :], preferred_element_type=F32)
    acc = acc + jnp.dot(att_ref[...], w_ref[RET_WIDTH:, :], preferred_element_type=F32)
    x1 = _layer_norm(DEEPNORM_ALPHA * x_ref[...] + acc, g_ref[...], b_ref[...])
    x1_ref[...] = x1
    x1p_ref[...] = _pack_rows(x1)
    x_hi = x1.astype(BF16)
    x_lo = (x1 - x_hi.astype(F32)).astype(BF16)
    rw = rw_ref[...]
    both = jnp.dot(x_hi, rw, preferred_element_type=F32)
    low = jnp.dot(x_lo, rw, preferred_element_type=F32)
    logits = both[:, :LANES] + both[:, LANES:] + low[:, :LANES]
    logits_t = logits.T[:N_EXPERTS, :]
    mask, comb = _route(logits_t, rb_ref[...])
    mask_ref[...] = mask
    comb_ref[...] = comb


def _outproj_ln_router(x2d, ret, att, w_bf, layer, ln_g, ln_b, rw_split, rb):
    t = x2d.shape[0]
    tm = min(OUTPROJ_TM, t)
    row = lambda w: pl.BlockSpec((tm, w), lambda i: (i, 0))
    const = lambda shape: pl.BlockSpec(shape, lambda i: (0,) * len(shape))
    w_spec = pl.BlockSpec((None, D_MODEL, D_MODEL), lambda i: (layer, 0, 0), pipeline_mode=pl.Buffered(1))
    lanes = pl.BlockSpec((N_EXPERTS, tm), lambda i: (0, i))
    return pl.pallas_call(
        _outproj_kernel,
        out_shape=(jax.ShapeDtypeStruct((t, D_MODEL), F32),
                   jax.ShapeDtypeStruct((t, HALF_ROW), U32),
                   jax.ShapeDtypeStruct((N_EXPERTS, t), F32),
                   jax.ShapeDtypeStruct((N_EXPERTS, t), F32)),
        grid=(t // tm,),
        in_specs=[row(D_MODEL), row(RET_WIDTH), row(ATT_WIDTH), w_spec,
                  const((1, D_MODEL)), const((1, D_MODEL)), const((D_MODEL, MXU_DIM)), const((N_EXPERTS, 1))],
        out_specs=(row(D_MODEL), row(HALF_ROW), lanes, lanes),
        compiler_params=_params(("parallel",)),
        name="outproj_ln_router",
    )(x2d, ret, att, w_bf, ln_g.reshape(1, D_MODEL), ln_b.reshape(1, D_MODEL), rw_split, rb.reshape(N_EXPERTS, 1))


def _split_router(router_w):
    hi = router_w.astype(BF16)
    lo = (router_w - hi.astype(F32)).astype(BF16)
    out = jnp.zeros((D_MODEL, MXU_DIM), BF16)
    return out.at[:, :N_EXPERTS].set(hi).at[:, LANES:LANES + N_EXPERTS].set(lo)


def _rank_kernel(mask_ref, rank_ref, count_ref, carry_ref):
    @pl.when(pl.program_id(0) == 0)
    def _():
        carry_ref[...] = jnp.zeros_like(carry_ref)

    m = mask_ref[...]
    tt = m.shape[1]
    upper = (lax.broadcasted_iota(I32, (tt, tt), 0) <= lax.broadcasted_iota(I32, (tt, tt), 1)).astype(BF16)
    incl = jnp.dot(m.astype(BF16), upper, preferred_element_type=F32)
    carry = carry_ref[...]
    rank_ref[...] = (carry[:, :1] + incl - m).astype(I32)
    carry = carry + jnp.sum(m, axis=1, keepdims=True)
    carry_ref[...] = carry
    count_ref[...] = carry


def _expert_ranks(mask_t):
    t = mask_t.shape[1]
    tt = min(RANK_TT, t)
    return pl.pallas_call(
        _rank_kernel,
        out_shape=(jax.ShapeDtypeStruct((N_EXPERTS, t), I32), jax.ShapeDtypeStruct((N_EXPERTS, LANES), F32)),
        grid=(t // tt,),
        in_specs=[pl.BlockSpec((N_EXPERTS, tt), lambda i: (0, i))],
        out_specs=(pl.BlockSpec((N_EXPERTS, tt), lambda i: (0, i)),
                   pl.BlockSpec((N_EXPERTS, LANES), lambda i: (0, 0))),
        scratch_shapes=[pltpu.VMEM((N_EXPERTS, LANES), F32)],
        compiler_params=_params(("arbitrary",)),
        name="expert_ranks",
    )(mask_t)


def _dispatch_plan(mask_t, rank_t, counts, comb_t, tm):
    t = mask_t.shape[1]
    n_tiles = 2 * t // tm + N_EXPERTS
    cnt = counts[:, 0].astype(I32)
    tiles = (cnt + tm - 1) // tm
    tile_end = jnp.cumsum(tiles)
    off = (tile_end - tiles) * tm
    sel = mask_t > 0.5
    pos_t = off[:, None] + rank_t
    pos_a = jnp.min(jnp.where(sel, pos_t, n_tiles * tm), axis=0)
    pos_b = jnp.max(jnp.where(sel, pos_t, -1), axis=0)
    e_id = jnp.arange(N_EXPERTS, dtype=I32)[:, None]
    e_a = jnp.min(jnp.where(sel, e_id, N_EXPERTS), axis=0)
    w_a = jnp.sum(jnp.where(sel & (e_id == e_a[None, :]), comb_t, 0.0), axis=0)
    w_b = jnp.sum(jnp.where(sel & (e_id != e_a[None, :]), comb_t, 0.0), axis=0)
    n_valid = tile_end[-1]
    tile_blk = jnp.minimum(jnp.arange(n_tiles, dtype=I32), n_valid - 1)
    tile_e = jnp.sum((tile_end[None, :] <= tile_blk[:, None]).astype(I32), axis=1)
    pad_end = (off + tiles * tm).at[N_EXPERTS - 1].set(n_tiles * tm)
    pad_rows = jnp.stack([off + cnt, pad_end]).astype(I32)
    return dict(pos_a=pos_a.astype(I32), pos_b=pos_b.astype(I32), w_a=w_a, w_b=w_b, n_valid=n_valid.astype(I32),
                tile_blk=tile_blk, tile_e=tile_e, pad_rows=pad_rows, n_tiles=n_tiles)


def _dispatch_kernel(pad_ref, pos_ref, x_ref, wg_ref, wu_ref, wd_ref, xs_hbm, wg_bf_ref, wu_bf_ref, wd_bf_ref,
                     zero_ref, sem):
    i = pl.program_id(0)
    tt = x_ref.shape[0]
    wg_bf_ref[...] = wg_ref[...].astype(BF16)
    wu_bf_ref[...] = wu_ref[...].astype(BF16)
    wd_bf_ref[...] = wd_ref[...].astype(BF16)

    def row_copy(src, dst):
        return pltpu.make_async_copy(src, dst, sem)

    @pl.when(i == 0)
    def _():
        zero_ref[...] = jnp.zeros_like(zero_ref)
        for e in range(N_EXPERTS):
            lo, hi = pad_ref[0, e], pad_ref[1, e]

            def start(r, c):
                row_copy(zero_ref.at[pl.ds(0, 1)], xs_hbm.at[pl.ds(r, 1)]).start()
                return c

            def wait(r, c):
                row_copy(zero_ref.at[pl.ds(0, 1)], xs_hbm.at[pl.ds(r, 1)]).wait()
                return c

            lax.fori_loop(lo, hi, start, 0)
            lax.fori_loop(lo, hi, wait, 0)

    def copies(j, u):
        src = x_ref.at[pl.ds(pl.multiple_of(j * SUBLANES, SUBLANES), SUBLANES)].at[pl.ds(u, 1)]
        k = j * SUBLANES + u
        return (row_copy(src, xs_hbm.at[pl.ds(pos_ref[0, 0, 2 * k], 1)]),
                row_copy(src, xs_hbm.at[pl.ds(pos_ref[0, 0, 2 * k + 1], 1)]))

    def start(j, c):
        for u in range(SUBLANES):
            a, b = copies(j, u)
            a.start(priority=0)
            b.start(priority=1)
        return c

    def wait(j, c):
        for u in range(SUBLANES):
            a, b = copies(j, u)
            a.wait()
            b.wait()
        return c

    lax.fori_loop(0, tt // SUBLANES, start, 0)
    lax.fori_loop(0, tt // SUBLANES, wait, 0)


def _dispatch(x1p, plan, pos, w_gate, w_up, w_down, layer):
    t = x1p.shape[0]
    n_steps, tt = pos.shape[0], pos.shape[2] // 2
    n_rows = plan["n_tiles"] * FFN_TM
    gu_rows = N_EXPERTS * D_MODEL // n_steps
    dn_rows = N_EXPERTS * D_FF // n_steps
    flat = lambda w: w.reshape(DEPTH, -1, w.shape[-1])
    w_in_spec = lambda rows, cols: pl.BlockSpec((None, rows, cols), lambda i, pad: (layer, i, 0))
    w_out_spec = lambda rows, cols: pl.BlockSpec((rows, cols), lambda i, pad: (i, 0))
    xs, wg_bf, wu_bf, wd_bf = pl.pallas_call(
        _dispatch_kernel,
        out_shape=(jax.ShapeDtypeStruct((n_rows, HALF_ROW), U32),
                   jax.ShapeDtypeStruct((N_EXPERTS * D_MODEL, D_FF), BF16),
                   jax.ShapeDtypeStruct((N_EXPERTS * D_MODEL, D_FF), BF16),
                   jax.ShapeDtypeStruct((N_EXPERTS * D_FF, D_MODEL), BF16)),
        grid_spec=pltpu.PrefetchScalarGridSpec(
            num_scalar_prefetch=1,
            grid=(n_steps,),
            in_specs=[pl.BlockSpec((1, 1, 2 * tt), lambda i, pad: (i, 0, 0), memory_space=pltpu.SMEM),
                      pl.BlockSpec((tt, HALF_ROW), lambda i, pad: (i, 0)),
                      w_in_spec(gu_rows, D_FF), w_in_spec(gu_rows, D_FF), w_in_spec(dn_rows, D_MODEL)],
            out_specs=(pl.BlockSpec(memory_space=pl.ANY),
                       w_out_spec(gu_rows, D_FF), w_out_spec(gu_rows, D_FF), w_out_spec(dn_rows, D_MODEL)),
            scratch_shapes=[pltpu.VMEM((SUBLANES, HALF_ROW), U32), pltpu.SemaphoreType.DMA(())]),
        compiler_params=_params(("arbitrary",)),
        name="dispatch",
    )(plan["pad_rows"], pos, x1p, flat(w_gate), flat(w_up), flat(w_down))
    return (xs, wg_bf.reshape(N_EXPERTS, D_MODEL, D_FF), wu_bf.reshape(N_EXPERTS, D_MODEL, D_FF),
            wd_bf.reshape(N_EXPERTS, D_FF, D_MODEL))


def _ffn_kernel(blk_ref, exp_ref, nvalid_ref, xs_ref, wg_ref, wu_ref, wd_ref, ys_ref):
    @pl.when(pl.program_id(0) >= nvalid_ref[0])
    def _():
        ys_ref[...] = jnp.zeros_like(ys_ref)

    @pl.when(pl.program_id(0) < nvalid_ref[0])
    def _():
        lo, hi = _unpack_rows(xs_ref[...])
        lo, hi = lo.astype(BF16), hi.astype(BF16)
        gate = jnp.dot(lo, wg_ref[:HALF_ROW, :], preferred_element_type=F32)
        gate = gate + jnp.dot(hi, wg_ref[HALF_ROW:, :], preferred_element_type=F32)
        up = jnp.dot(lo, wu_ref[:HALF_ROW, :], preferred_element_type=F32)
        up = up + jnp.dot(hi, wu_ref[HALF_ROW:, :], preferred_element_type=F32)
        hidden = (gate * jax.nn.sigmoid(gate) * up).astype(BF16)
        ys_ref[...] = _pack_rows(jnp.dot(hidden, wd_ref[...], preferred_element_type=F32))


def _expert_ffn(xs, plan, wg, wu, wd):
    tm = FFN_TM
    n_tiles = plan["n_tiles"]
    rows = pl.BlockSpec((tm, HALF_ROW), lambda i, blk, e, n: (blk[i], 0))
    w_spec = lambda k, n: pl.BlockSpec((None, k, n), lambda i, blk, e, nv: (e[i], 0, 0))
    return pl.pallas_call(
        _ffn_kernel,
        out_shape=jax.ShapeDtypeStruct((n_tiles * tm, HALF_ROW), U32),
        grid_spec=pltpu.PrefetchScalarGridSpec(
            num_scalar_prefetch=3,
            grid=(n_tiles,),
            in_specs=[rows, w_spec(D_MODEL, D_FF), w_spec(D_MODEL, D_FF), w_spec(D_FF, D_MODEL)],
            out_specs=pl.BlockSpec((tm, HALF_ROW), lambda i, blk, e, n: (i, 0))),
        compiler_params=_params(("arbitrary",)),
        name="expert_ffn",
    )(plan["tile_blk"], plan["tile_e"], plan["n_valid"].reshape(1), xs, wg, wu, wd)


def _combine_kernel(pos_ref, next_pos_ref, x1_ref, w_ref, g_ref, b_ref, ys_hbm, x2_ref, buf_ref, sem):
    i = pl.program_id(0)
    tt = x1_ref.shape[0]
    slot = i % 2

    def copy(p_ref, j, u, which, s):
        dst = buf_ref.at[s, which, pl.ds(pl.multiple_of(j * SUBLANES, SUBLANES), SUBLANES)].at[pl.ds(u, 1)]
        k = j * SUBLANES + u
        return pltpu.make_async_copy(ys_hbm.at[pl.ds(p_ref[0, 0, 2 * k + which], 1)], dst, sem.at[s])

    def gather(p_ref, s):
        def start(j, c):
            for u in range(SUBLANES):
                copy(p_ref, j, u, 0, s).start(priority=0)
                copy(p_ref, j, u, 1, s).start(priority=1)
            return c
        lax.fori_loop(0, tt // SUBLANES, start, 0)

    @pl.when(i == 0)
    def _():
        gather(pos_ref, slot)

    @pl.when(i + 1 < pl.num_programs(0))
    def _():
        gather(next_pos_ref, 1 - slot)

    def wait(j, c):
        for u in range(SUBLANES):
            copy(pos_ref, j, u, 0, slot).wait()
            copy(pos_ref, j, u, 1, slot).wait()
        return c

    lax.fori_loop(0, tt // SUBLANES, wait, 0)
    a_lo, a_hi = _unpack_rows(buf_ref[slot, 0])
    b_lo, b_hi = _unpack_rows(buf_ref[slot, 1])
    w = w_ref[...]
    w_a, w_b = w[:, 0:1], w[:, 1:2]
    moe = jnp.concatenate([w_a * a_lo + w_b * b_lo, w_a * a_hi + w_b * b_hi], axis=1)
    x2_ref[...] = _layer_norm(DEEPNORM_ALPHA * x1_ref[...] + moe, g_ref[...], b_ref[...])


def _combine_ln(x1, ys, plan, pos, ln_g, ln_b):
    t = x1.shape[0]
    n_steps, tt = pos.shape[0], pos.shape[2] // 2
    w = jnp.concatenate([plan["w_a"][:, None], plan["w_b"][:, None], jnp.zeros((t, LANES - 2), F32)], axis=1)
    const = lambda shape: pl.BlockSpec(shape, lambda i: (0,) * len(shape))
    return pl.pallas_call(
        _combine_kernel,
        out_shape=jax.ShapeDtypeStruct((t, D_MODEL), F32),
        grid=(n_steps,),
        in_specs=[pl.BlockSpec((1, 1, 2 * tt), lambda i: (i, 0, 0), memory_space=pltpu.SMEM),
                  pl.BlockSpec((1, 1, 2 * tt), lambda i: (jnp.minimum(i + 1, n_steps - 1), 0, 0),
                               memory_space=pltpu.SMEM),
                  pl.BlockSpec((tt, D_MODEL), lambda i: (i, 0)),
                  pl.BlockSpec((tt, LANES), lambda i: (i, 0)),
                  const((1, D_MODEL)), const((1, D_MODEL)),
                  pl.BlockSpec(memory_space=pl.ANY)],
        out_specs=pl.BlockSpec((tt, D_MODEL), lambda i: (i, 0)),
        scratch_shapes=[pltpu.VMEM((2, 2, tt, HALF_ROW), U32), pltpu.SemaphoreType.DMA((2,))],
        compiler_params=_params(("arbitrary",)),
        name="combine_ln",
    )(pos, pos, x1, w, ln_g.reshape(1, D_MODEL), ln_b.reshape(1, D_MODEL), ys)


def kernel(x, w_in, ret_norm_gain, rel_bias, w_out, ln1_g, ln1_b, router_w, router_b,
           w_gate, w_up, w_down, ln2_g, ln2_b):
    batch, seq, _ = x.shape
    t = batch * seq
    tabs = _retention_tables(seq)
    rw_split = _split_router(router_w)
    h = x.reshape(t, D_MODEL)
    tt_d, tt_c = min(DISPATCH_TT, t), min(COMBINE_TT, t)
    assert (N_EXPERTS * D_FF) % (t // tt_d) == 0, "each dispatch step converts an equal slice of the expert weights"
    w_in_bf, w_out_bf = w_in.astype(BF16), w_out.astype(BF16)
    for l in range(DEPTH):
        proj = _inproj(h, w_in_bf, l)
        ret = _retention(proj, tabs, ret_norm_gain[l], batch, seq)
        att = _band_attention(proj, _band_bias(rel_bias[l]), batch, seq)
        x1, x1p, mask_t, comb_t = _outproj_ln_router(h, ret, att, w_out_bf, l, ln1_g[l], ln1_b[l],
                                                     rw_split, router_b)
        rank_t, counts = _expert_ranks(mask_t)
        plan = _dispatch_plan(mask_t, rank_t, counts, comb_t, FFN_TM)
        pos = jnp.stack([plan["pos_a"], plan["pos_b"]], axis=1)
        xs, wg_bf, wu_bf, wd_bf = _dispatch(x1p, plan, pos.reshape(t // tt_d, 1, 2 * tt_d), w_gate, w_up, w_down, l)
        ys = _expert_ffn(xs, plan, wg_bf, wu_bf, wd_bf)
        h = _combine_ln(x1, ys, plan, pos.reshape(t // tt_c, 1, 2 * tt_c), ln2_g[l], ln2_b[l])
    return h.reshape(batch, seq, D_MODEL)
```
